```python
import math
import jax, jax.numpy as jnp
from jax import lax
import numpy as np

D_MODEL = 1024
BATCH = 8
SEQ = 2048
DEPTH = 4

N_HEADS = 8
HEAD_DIM = 64
V_HEAD_DIM = 2 * HEAD_DIM
QK_WIDTH = 2 * N_HEADS * HEAD_DIM
V_WIDTH = N_HEADS * V_HEAD_DIM
CONV_WIDTH = D_MODEL
CONV_GROUPS = 8
CONV_K = 3
SPLIT_SIZES = (QK_WIDTH, QK_WIDTH, V_WIDTH, CONV_WIDTH, CONV_WIDTH, CONV_WIDTH, D_MODEL, D_MODEL)
SPLIT_IDX = (1024, 2048, 3072, 4096, 5120, 6144, 7168)
IN_WIDTH = 8192
D_FF = 2816
N_EXPERTS = 8
TOP_K = 2
D_FF_EXPERT = 3584
N_DENSE = (DEPTH + 1) // 2
N_MOE = DEPTH // 2
ROPE_THETA = 10000.0
RMS_EPS = 1e-5
Q_BLOCK = 128

kernel_name = "hybrid_diffattn_shortconv_moe_trunk"


def rmsnorm(x, g):
    xf = x.astype(jnp.float32)
    y = xf * lax.rsqrt(jnp.mean(xf * xf, axis=-1, keepdims=True) + RMS_EPS)
    return (y * g.astype(jnp.float32)).astype(x.dtype)


def rope_tables(T, dtype):
    pos = jnp.arange(T, dtype=jnp.float32)
    inv = 1.0 / (ROPE_THETA ** (jnp.arange(0, HEAD_DIM, 2, dtype=jnp.float32) / HEAD_DIM))
    ang = pos[:, None] * inv[None, :]
    ang = jnp.concatenate([ang, ang], axis=-1)
    return jnp.cos(ang).astype(dtype), jnp.sin(ang).astype(dtype)


def apply_rope(x, cos, sin):
    x1, x2 = jnp.split(x, 2, axis=-1)
    rot = jnp.concatenate([-x2, x1], axis=-1)
    return x * cos[None, :, None, :] + rot * sin[None, :, None, :]


def diff_attention(q, k, v, lam, lam_init, subln_g):
    B, T = q.shape[0], q.shape[1]
    q = q * (HEAD_DIM ** -0.5)
    neg = jnp.finfo(jnp.float32).min
    outs = []
    for i in range(T // Q_BLOCK):
        q0 = i * Q_BLOCK
        kv_len = q0 + Q_BLOCK
        q_blk = q[:, q0:kv_len]
        k_blk = k[:, :kv_len]
        v_blk = v[:, :kv_len]
        s = jnp.einsum('bqhmd,bkhmd->bhmqk', q_blk, k_blk).astype(jnp.float32)
        q_pos = q0 + jnp.arange(Q_BLOCK)
        k_pos = jnp.arange(kv_len)
        mask = k_pos[None, :] <= q_pos[:, None]
        s = jnp.where(mask[None, None, None], s, neg)
        p = jax.nn.softmax(s, axis=-1)
        p_diff = p[:, :, 0] - lam * p[:, :, 1]
        outs.append(jnp.einsum('bhqk,bkhe->bqhe', p_diff.astype(v.dtype), v_blk))
    o = jnp.concatenate(outs, axis=1)
    o = rmsnorm(o, subln_g) * (1.0 - lam_init)
    return o.reshape(B, T, V_WIDTH)


def causal_short_conv(u, w):
    T = u.shape[1]
    up = jnp.pad(u, ((0, 0), (CONV_K - 1, 0), (0, 0)))
    y = w[0] * up[:, 0:T]
    for j in range(1, CONV_K):
        y = y + w[j] * up[:, j:j + T]
    return y


def hybrid_mixer(h, w_in, conv_w, lam_q1, lam_k1, lam_q2, lam_k2, subln_g, w_out,
                 layer_idx, cos, sin):
    B, T, _ = h.shape
    proj = h @ w_in
    q, k, v, b_g, c_g, u, g_attn, g_conv = jnp.split(proj, SPLIT_IDX, axis=-1)
    q = apply_rope(q.reshape(B, T, 2 * N_HEADS, HEAD_DIM), cos, sin)
    k = apply_rope(k.reshape(B, T, 2 * N_HEADS, HEAD_DIM), cos, sin)
    q = q.reshape(B, T, N_HEADS, 2, HEAD_DIM)
    k = k.reshape(B, T, N_HEADS, 2, HEAD_DIM)
    v = v.reshape(B, T, N_HEADS, V_HEAD_DIM)
    lam_init = 0.8 - 0.6 * math.exp(-0.3 * layer_idx)
    lam = (jnp.exp(jnp.sum(lam_q1.astype(jnp.float32) * lam_k1.astype(jnp.float32)))
           - jnp.exp(jnp.sum(lam_q2.astype(jnp.float32) * lam_k2.astype(jnp.float32)))
           + lam_init)
    y_attn = diff_attention(q, k, v, lam, lam_init, subln_g)
    y_conv = b_g * causal_short_conv(c_g * u, conv_w)
    m = jax.nn.sigmoid(g_attn) * y_attn + jax.nn.sigmoid(g_conv) * y_conv
    return m @ w_out


def swiglu(h, w_gate, w_up, w_down):
    return (jax.nn.silu(h @ w_gate) * (h @ w_up)) @ w_down


def moe_swiglu(h, w_router, e_gate, e_up, e_down):
    B, T, D = h.shape
    xt = h.reshape(B * T, D)
    logits = (xt @ w_router).astype(jnp.float32)
    top_vals, top_idx = lax.top_k(logits, TOP_K)
    top_w = jax.nn.softmax(top_vals, axis=-1)
    gates = jnp.sum(jax.nn.one_hot(top_idx, N_EXPERTS, dtype=jnp.float32) * top_w[..., None],
                    axis=1).astype(h.dtype)
    y = jnp.zeros_like(xt)
    for e in range(N_EXPERTS):
        y = y + gates[:, e:e + 1] * swiglu(xt, e_gate[e], e_up[e], e_down[e])
    return y.reshape(B, T, D)


def setup_inputs(seed: int = 0) -> dict:
    key = jax.random.key(seed)
    ks = jax.random.split(key, 24)
    f32 = jnp.float32
    nrm = lambda k, shape, scale: jax.random.normal(k, shape, f32) * scale
    return {
        "x": nrm(ks[0], (BATCH, SEQ, D_MODEL), 1.0),
        "norm_mix": 1.0 + nrm(ks[1], (DEPTH, D_MODEL), 0.02),
        "w_in": nrm(ks[2], (DEPTH, D_MODEL, IN_WIDTH), D_MODEL ** -0.5),
        "conv_w": nrm(ks[3], (DEPTH, CONV_K, CONV_WIDTH), CONV_K ** -0.5),
        "lam_q1": nrm(ks[4], (DEPTH, HEAD_DIM), 0.1),
        "lam_k1": nrm(ks[5], (DEPTH, HEAD_DIM), 0.1),
        "lam_q2": nrm(ks[6], (DEPTH, HEAD_DIM), 0.1),
        "lam_k2": nrm(ks[7], (DEPTH, HEAD_DIM), 0.1),
        "subln_g": 1.0 + nrm(ks[8], (DEPTH, V_HEAD_DIM), 0.02),
        "w_out": nrm(ks[9], (DEPTH, D_MODEL, D_MODEL), D_MODEL ** -0.5),
        "norm_ffn": 1.0 + nrm(ks[10], (DEPTH, D_MODEL), 0.02),
        "ffn_w_gate": nrm(ks[11], (N_DENSE, D_MODEL, D_FF), D_MODEL ** -0.5),
        "ffn_w_up": nrm(ks[12], (N_DENSE, D_MODEL, D_FF), D_MODEL ** -0.5),
        "ffn_w_down": nrm(ks[13], (N_DENSE, D_FF, D_MODEL), D_FF ** -0.5),
        "router_w": nrm(ks[14], (N_MOE, D_MODEL, N_EXPERTS), D_MODEL ** -0.5),
        "exp_w_gate": nrm(ks[15], (N_MOE, N_EXPERTS, D_MODEL, D_FF_EXPERT), D_MODEL ** -0.5),
        "exp_w_up": nrm(ks[16], (N_MOE, N_EXPERTS, D_MODEL, D_FF_EXPERT), D_MODEL ** -0.5),
        "exp_w_down": nrm(ks[17], (N_MOE, N_EXPERTS, D_FF_EXPERT, D_MODEL), D_FF_EXPERT ** -0.5),
        "norm_final": 1.0 + nrm(ks[18], (D_MODEL,), 0.02),
    }


def reference(x, norm_mix, w_in, conv_w, lam_q1, lam_k1, lam_q2, lam_k2, subln_g, w_out,
              norm_ffn, ffn_w_gate, ffn_w_up, ffn_w_down, router_w, exp_w_gate, exp_w_up,
              exp_w_down, norm_final):
    T = x.shape[1]
    cos, sin = rope_tables(T, x.dtype)
    for l in range(DEPTH):
        h = rmsnorm(x, norm_mix[l])
        x = x + hybrid_mixer(h, w_in[l], conv_w[l], lam_q1[l], lam_k1[l], lam_q2[l], lam_k2[l],
                             subln_g[l], w_out[l], l, cos, sin)
        h = rmsnorm(x, norm_ffn[l])
        j = l // 2
        if l % 2 == 0:
            x = x + swiglu(h, ffn_w_gate[j], ffn_w_up[j], ffn_w_down[j])
        else:
            x = x + moe_swiglu(h, router_w[j], exp_w_gate[j], exp_w_up[j], exp_w_down[j])
    return rmsnorm(x, norm_final)
```

```python
import functools
import math

import jax
import jax.numpy as jnp
from jax import lax
from jax.experimental import pallas as pl
from jax.experimental.pallas import tpu as pltpu

F32 = jnp.float32
BF16 = jnp.bfloat16
I32 = jnp.int32

HEAD_DIM = 64
V_HEAD_DIM = 2 * HEAD_DIM
N_EXPERTS = 8
CONV_K = 3
ROPE_THETA = 10000.0
RMS_EPS = 1e-5
LANES = 128
MIB = 1024 * 1024
VMEM_LIMIT = 56 * MIB


def _tile(dim, pref):
    if dim <= pref:
        return dim
    t = pref
    while t >= LANES:
        if dim % t == 0:
            return t
        t -= LANES
    return dim


def _params(sem):
    return pltpu.CompilerParams(dimension_semantics=sem, vmem_limit_bytes=VMEM_LIMIT)


def _rmsnorm_bf16(x, g):
    var = jnp.mean(x * x, axis=-1, keepdims=True)
    return (x * lax.rsqrt(var + RMS_EPS) * g).astype(BF16)


def _qkv_kernel(x_ref, g_ref, w_ref, cos_ref, sin_ref, o_ref, h_ref, *, n_q, n_rope):
    j = pl.program_id(1)

    @pl.when(j == 0)
    def _():
        h_ref[...] = _rmsnorm_bf16(x_ref[...], g_ref[...])

    y = jnp.dot(h_ref[...], w_ref[...], preferred_element_type=F32)
    tm, tn = y.shape

    @pl.when(j < n_rope)
    def _():
        scale = jnp.where(j < n_q, HEAD_DIM ** -0.5, 1.0).astype(F32)
        cos = cos_ref[...]
        sin = sin_ref[...]
        lane = lax.broadcasted_iota(I32, (tm, LANES), 1)
        first = (lane & (HEAD_DIM // 2)) == 0
        for c in range(tn // LANES):
            yc = y[:, c * LANES:(c + 1) * LANES]
            rot = jnp.where(first,
                            pltpu.roll(yc, LANES - HEAD_DIM // 2, 1),
                            pltpu.roll(yc, HEAD_DIM // 2, 1))
            o_ref[:, c * LANES:(c + 1) * LANES] = ((yc * cos + rot * sin) * scale).astype(BF16)

    @pl.when(j >= n_rope)
    def _():
        o_ref[...] = y.astype(BF16)


def _qkv_proj(x2, g, w_qkv, cos_t, sin_t, seq):
    n, d = x2.shape
    width = w_qkv.shape[1]
    tm = _tile(seq, 1024)
    tn = _tile(d, 512)
    n_q = d // tn
    kern = functools.partial(_qkv_kernel, n_q=n_q, n_rope=2 * n_q)
    t_blocks = seq // tm
    return pl.pallas_call(
        kern,
        grid=(n // tm, width // tn),
        in_specs=[
            pl.BlockSpec((tm, d), lambda i, j: (i, 0)),
            pl.BlockSpec((1, d), lambda i, j: (0, 0)),
            pl.BlockSpec((d, tn), lambda i, j: (0, j)),
            pl.BlockSpec((tm, LANES), lambda i, j: (i % t_blocks, 0)),
            pl.BlockSpec((tm, LANES), lambda i, j: (i % t_blocks, 0)),
        ],
        out_specs=pl.BlockSpec((tm, tn), lambda i, j: (i, j)),
        out_shape=jax.ShapeDtypeStruct((n, width), BF16),
        scratch_shapes=[pltpu.VMEM((tm, d), BF16)],
        compiler_params=_params(("arbitrary", "arbitrary")),
        name="qkv_rope",
    )(x2, g, w_qkv, cos_t, sin_t)


def _conv_kernel(x_ref, g_ref, wb_ref, wc_ref, wu_ref, wga_ref, wgc_ref, cw_ref,
                 zc_ref, sga_ref, h_ref):
    c = pl.program_id(1)

    @pl.when(c == 0)
    def _():
        h_ref[...] = _rmsnorm_bf16(x_ref[...], g_ref[...])

    h = h_ref[...]
    dot = functools.partial(jnp.dot, preferred_element_type=F32)
    cu = dot(h, wc_ref[...]) * dot(h, wu_ref[...])
    row = lax.broadcasted_iota(I32, cu.shape, 0)
    cu1 = jnp.where(row >= 1, pltpu.roll(cu, 1, 0), 0.0)
    cu2 = jnp.where(row >= 2, pltpu.roll(cu, 2, 0), 0.0)
    cw = cw_ref[...]
    y = cw[0:1, :] * cu2
    y = y + cw[1:2, :] * cu1
    y = y + cw[2:3, :] * cu
    y_conv = dot(h, wb_ref[...]) * y
    zc_ref[...] = (jax.nn.sigmoid(dot(h, wgc_ref[...])) * y_conv).astype(BF16)
    sga_ref[...] = jax.nn.sigmoid(dot(h, wga_ref[...])).astype(BF16)


def _conv_branch(x2, g, w_in, conv_w, seq):
    n, d = x2.shape
    tn = _tile(d, 256)
    nc = d // tn

    def wspec(k):
        return pl.BlockSpec((d, tn), lambda b, c: (0, k * nc + c))

    out = jax.ShapeDtypeStruct((n, d), BF16)
    return pl.pallas_call(
        _conv_kernel,
        grid=(n // seq, nc),
        in_specs=[
            pl.BlockSpec((seq, d), lambda b, c: (b, 0)),
            pl.BlockSpec((1, d), lambda b, c: (0, 0)),
            wspec(3), wspec(4), wspec(5), wspec(6), wspec(7),
            pl.BlockSpec((CONV_K, tn), lambda b, c: (0, c)),
        ],
        out_specs=[pl.BlockSpec((seq, tn), lambda b, c: (b, c)),
                   pl.BlockSpec((seq, tn), lambda b, c: (b, c))],
        out_shape=[out, out],
        scratch_shapes=[pltpu.VMEM((seq, d), BF16)],
        compiler_params=_params(("arbitrary", "arbitrary")),
        name="conv_gate",
    )(x2, g, w_in, w_in, w_in, w_in, w_in, conv_w)


def _attn_kernel(lq1_ref, lk1_ref, lq2_ref, lk2_ref, q_ref, k_ref, v_ref, g_ref, o_ref,
                 *, tq, lam_init):
    qi = pl.program_id(2)
    lam = (jnp.exp(jnp.sum(lq1_ref[...] * lk1_ref[...], axis=-1, keepdims=True))
           - jnp.exp(jnp.sum(lq2_ref[...] * lk2_ref[...], axis=-1, keepdims=True))
           + lam_init)

    q = q_ref[...]
    lane = lax.broadcasted_iota(I32, q.shape, 1)
    zero = jnp.zeros_like(q)
    q_maps = (jnp.where(lane < HEAD_DIM, q, zero), jnp.where(lane >= HEAD_DIM, q, zero))
    nt = (((1,), (1,)), ((), ()))

    def scores(j):
        k = k_ref[pl.ds(pl.multiple_of(j * tq, tq), tq), :]
        return [lax.dot_general(qm, k, nt, preferred_element_type=F32) for qm in q_maps]

    def update(state, s, v):
        m, l, acc = state
        m_new = jnp.maximum(m, jnp.max(s, axis=-1, keepdims=True))
        alpha = jnp.exp(m - m_new)
        p = jnp.exp(s - m_new)
        l = alpha * l + jnp.sum(p, axis=-1, keepdims=True)
        acc = alpha * acc + jnp.dot(p.astype(BF16), v, preferred_element_type=F32)
        return m_new, l, acc

    def body(j, states):
        v = v_ref[pl.ds(pl.multiple_of(j * tq, tq), tq), :]
        return tuple(update(st, s, v) for st, s in zip(states, scores(j)))

    init = (jnp.full((tq, 1), -jnp.inf, F32), jnp.zeros((tq, 1), F32),
            jnp.zeros((tq, V_HEAD_DIM), F32))
    states = lax.fori_loop(0, qi, body, (init, init))

    row = lax.broadcasted_iota(I32, (tq, tq), 0)
    col = lax.broadcasted_iota(I32, (tq, tq), 1)
    neg = jnp.finfo(F32).min
    v = v_ref[pl.ds(pl.multiple_of(qi * tq, tq), tq), :]
    states = tuple(update(st, jnp.where(col <= row, s, neg), v)
                   for st, s in zip(states, scores(qi)))

    (_, l1, a1), (_, l2, a2) = states
    o = a1 / l1 - lam * (a2 / l2)
    var = jnp.mean(o * o, axis=-1, keepdims=True)
    y = (o * lax.rsqrt(var + RMS_EPS) * g_ref[...]) * (1.0 - lam_init)
    o_ref[...] = y.astype(BF16)


def _attention(qkv, lq1, lk1, lq2, lk2, subln_g, batch, seq, n_heads, lam_init):
    n = qkv.shape[0]
    tq = _tile(seq, 256)
    nq = seq // tq
    kern = functools.partial(_attn_kernel, tq=tq, lam_init=lam_init)
    lspec = pl.BlockSpec((1, HEAD_DIM), lambda b, h, i: (0, 0))
    return pl.pallas_call(
        kern,
        grid=(batch, n_heads, nq),
        in_specs=[
            lspec, lspec, lspec, lspec,
            pl.BlockSpec((tq, V_HEAD_DIM), lambda b, h, i: (b * nq + i, h)),
            pl.BlockSpec((seq, V_HEAD_DIM), lambda b, h, i: (b, n_heads + h)),
            pl.BlockSpec((seq, V_HEAD_DIM), lambda b, h, i: (b, 2 * n_heads + h)),
            pl.BlockSpec((1, V_HEAD_DIM), lambda b, h, i: (0, 0)),
        ],
        out_specs=pl.BlockSpec((tq, V_HEAD_DIM), lambda b, h, i: (b * nq + i, h)),
        out_shape=jax.ShapeDtypeStruct((n, n_heads * V_HEAD_DIM), BF16),
        compiler_params=_params(("arbitrary", "arbitrary", "arbitrary")),
        name="diff_attention",
    )(lq1, lk1, lq2, lk2, qkv, qkv, qkv, subln_g)


def _outproj_kernel(x_ref, ya_ref, sga_ref, zc_ref, w_ref, o_ref):
    m = sga_ref[...].astype(F32) * ya_ref[...].astype(F32) + zc_ref[...].astype(F32)
    o_ref[...] = x_ref[...] + jnp.dot(m.astype(BF16), w_ref[...], preferred_element_type=F32)


def _out_proj(x2, y_attn, sga, zc, w_out):
    n, d = x2.shape
    tm = _tile(n, 512)
    row = pl.BlockSpec((tm, d), lambda i: (i, 0))
    return pl.pallas_call(
        _outproj_kernel,
        grid=(n // tm,),
        in_specs=[row, row, row, row, pl.BlockSpec((d, d), lambda i: (0, 0))],
        out_specs=row,
        out_shape=jax.ShapeDtypeStruct((n, d), F32),
        compiler_params=_params(("arbitrary",)),
        name="out_proj",
    )(x2, y_attn, sga, zc, w_out)


def _ffn_kernel(x_ref, g_ref, wg_ref, wu_ref, wd_ref, o_ref, h_ref, acc_ref, *, nf):
    f = pl.program_id(1)

    @pl.when(f == 0)
    def _():
        h_ref[...] = _rmsnorm_bf16(x_ref[...], g_ref[...])
        acc_ref[...] = jnp.zeros_like(acc_ref)

    h = h_ref[...]
    gate = jnp.dot(h, wg_ref[...], preferred_element_type=F32)
    up = jnp.dot(h, wu_ref[...], preferred_element_type=F32)
    a = (gate * jax.nn.sigmoid(gate) * up).astype(BF16)
    acc_ref[...] += jnp.dot(a, wd_ref[...], preferred_element_type=F32)

    @pl.when(f == nf - 1)
    def _():
        o_ref[...] = x_ref[...] + acc_ref[...]


def _ffn_dense(x2, g, w_gate, w_up, w_down):
    n, d = x2.shape
    ff = w_gate.shape[1]
    tm = _tile(n, 1024)
    tf = _tile(ff, 256)
    nf = ff // tf
    return pl.pallas_call(
        functools.partial(_ffn_kernel, nf=nf),
        grid=(n // tm, nf),
        in_specs=[
            pl.BlockSpec((tm, d), lambda i, f: (i, 0)),
            pl.BlockSpec((1, d), lambda i, f: (0, 0)),
            pl.BlockSpec((d, tf), lambda i, f: (0, f)),
            pl.BlockSpec((d, tf), lambda i, f: (0, f)),
            pl.BlockSpec((tf, d), lambda i, f: (f, 0)),
        ],
        out_specs=pl.BlockSpec((tm, d), lambda i, f: (i, 0)),
        out_shape=jax.ShapeDtypeStruct((n, d), F32),
        scratch_shapes=[pltpu.VMEM((tm, d), BF16), pltpu.VMEM((tm, d), F32)],
        compiler_params=_params(("arbitrary", "arbitrary")),
        name="ffn_dense",
    )(x2, g, w_gate, w_up, w_down)


_R_E1, _R_E2, _R_RANK1, _R_RANK2, _R_W1, _R_W2 = range(6)


def _router_kernel(x_ref, g_ref, whi_ref, wlo_ref, h_ref, info_ref, cnt_ref, tri_ref, carry_ref):
    i = pl.program_id(0)
    tm = x_ref.shape[0]

    @pl.when(i == 0)
    def _():
        r = lax.broadcasted_iota(I32, (tm, tm), 0)
        c = lax.broadcasted_iota(I32, (tm, tm), 1)
        tri_ref[...] = jnp.where(c < r, 1.0, 0.0).astype(BF16)
        carry_ref[...] = jnp.zeros_like(carry_ref)

    x = x_ref[...]
    var = jnp.mean(x * x, axis=-1, keepdims=True)
    h = x * lax.rsqrt(var + RMS_EPS) * g_ref[...]
    h_ref[...] = h

    h_hi = h.astype(BF16)
    h_lo = (h - h_hi.astype(F32)).astype(BF16)
    dot = functools.partial(jnp.dot, preferred_element_type=F32)
    logits = dot(h_hi, whi_ref[...]) + dot(h_lo, whi_ref[...]) + dot(h_hi, wlo_ref[...])

    lane = lax.broadcasted_iota(I32, (tm, LANES), 1)
    ninf = -jnp.inf
    lg = jnp.where(lane < N_EXPERTS, logits, ninf)
    m1 = jnp.max(lg, axis=-1, keepdims=True)
    e1 = jnp.min(jnp.where(lg == m1, lane, LANES), axis=-1, keepdims=True)
    lg2 = jnp.where(lane == e1, ninf, lg)
    m2 = jnp.max(lg2, axis=-1, keepdims=True)
    e2 = jnp.min(jnp.where(lg2 == m2, lane, LANES), axis=-1, keepdims=True)
    ex2 = jnp.exp(m2 - m1)
    den = 1.0 + ex2
    w1 = 1.0 / den
    w2 = ex2 / den

    oh1 = lane == e1
    oh2 = lane == e2
    oh = jnp.where(oh1 | oh2, 1.0, 0.0)
    carry = carry_ref[0:1, :]
    before = dot(tri_ref[...], oh.astype(BF16)) + carry
    rank1 = jnp.sum(jnp.where(oh1, before, 0.0), axis=-1, keepdims=True)
    rank2 = jnp.sum(jnp.where(oh2, before, 0.0), axis=-1, keepdims=True)
    new_carry = carry + jnp.sum(oh, axis=0, keepdims=True)
    carry_ref[...] = jnp.broadcast_to(new_carry, carry_ref.shape)
    cnt_ref[...] = jnp.broadcast_to(new_carry, cnt_ref.shape)

    info = jnp.zeros((tm, LANES), F32)
    for k, val in ((_R_E1, e1.astype(F32)), (_R_E2, e2.astype(F32)), (_R_RANK1, rank1),
                   (_R_RANK2, rank2), (_R_W1, w1), (_R_W2, w2)):
        info = jnp.where(lane == k, val, info)
    info_ref[...] = info


def _router(x2, g, w_hi, w_lo):
    n, d = x2.shape
    tm = _tile(n, 1024)
    return pl.pallas_call(
        _router_kernel,
        grid=(n // tm,),
        in_specs=[
            pl.BlockSpec((tm, d), lambda i: (i, 0)),
            pl.BlockSpec((1, d), lambda i: (0, 0)),
            pl.BlockSpec((d, LANES), lambda i: (0, 0)),
            pl.BlockSpec((d, LANES), lambda i: (0, 0)),
        ],
        out_specs=[
            pl.BlockSpec((tm, d), lambda i: (i, 0)),
            pl.BlockSpec((tm, LANES), lambda i: (i, 0)),
            pl.BlockSpec((8, LANES), lambda i: (0, 0)),
        ],
        out_shape=[
            jax.ShapeDtypeStruct((n, d), F32),
            jax.ShapeDtypeStruct((n, LANES), F32),
            jax.ShapeDtypeStruct((8, LANES), F32),
        ],
        scratch_shapes=[pltpu.VMEM((tm, tm), BF16), pltpu.VMEM((8, LANES), F32)],
        compiler_params=_params(("arbitrary",)),
        name="moe_router",
    )(x2, g, w_hi, w_lo)


def _dispatch_kernel(fill_ref, pos_ref, h_hbm, xs_hbm, zero_ref, sem, *, tb, tm, n_steps, rows):
    i = pl.program_id(0)
    base = i * tb

    def row_copy(src_row, dst_row):
        return pltpu.make_async_copy(h_hbm.at[pl.ds(src_row, 1), :],
                                     xs_hbm.at[pl.ds(dst_row, 1), :], sem)

    def issue(r, carry):
        row_copy(base + r, pos_ref[0, r]).start()
        row_copy(base + r, pos_ref[0, tb + r]).start()
        return carry

    lax.fori_loop(0, tb, issue, 0)

    def drain(r, carry):
        row_copy(0, 0).wait()
        row_copy(0, 0).wait()
        return carry

    lax.fori_loop(0, tb, drain, 0)

    @pl.when(i == n_steps - 1)
    def _():
        zero_ref[...] = jnp.zeros_like(zero_ref)

        def zero_row(dst_row):
            return pltpu.make_async_copy(zero_ref.at[pl.ds(0, 1), :],
                                         xs_hbm.at[pl.ds(dst_row, 1), :], sem)

        def zero_tile(t):
            return pltpu.make_async_copy(
                zero_ref, xs_hbm.at[pl.ds(pl.multiple_of(t * tm, tm), tm), :], sem)

        for e in range(N_EXPERTS):
            lo = fill_ref[2 * e]
            hi = fill_ref[2 * e + 1]
            lax.fori_loop(lo, hi, lambda r, c: (zero_row(r).start(), c)[1], 0)
            lax.fori_loop(lo, hi, lambda r, c: (zero_row(0).wait(), c)[1], 0)
        first_unused = fill_ref[2 * N_EXPERTS]
        lax.fori_loop(first_unused, rows // tm, lambda t, c: (zero_tile(t).start(), c)[1], 0)
        lax.fori_loop(first_unused, rows // tm, lambda t, c: (zero_tile(0).wait(), c)[1], 0)


def _dispatch(fill, pos_blocks, h2, rows, tm):
    n, d = h2.shape
    nblk, _, two_tb = pos_blocks.shape
    tb = two_tb // 2
    kern = functools.partial(_dispatch_kernel, tb=tb, tm=tm, n_steps=nblk, rows=rows)
    return pl.pallas_call(
        kern,
        grid_spec=pltpu.PrefetchScalarGridSpec(
            num_scalar_prefetch=1,
            grid=(nblk,),
            in_specs=[
                pl.BlockSpec((None, 1, two_tb), lambda i, fill: (i, 0, 0), memory_space=pltpu.SMEM),
                pl.BlockSpec(memory_space=pl.ANY),
            ],
            out_specs=pl.BlockSpec(memory_space=pl.ANY),
            scratch_shapes=[pltpu.VMEM((tm, d), F32), pltpu.SemaphoreType.DMA(())],
        ),
        out_shape=jax.ShapeDtypeStruct((rows, d), F32),
        compiler_params=_params(("arbitrary",)),
        name="moe_dispatch",
    )(fill, pos_blocks, h2)


def _experts_kernel(te_ref, tblk_ref, tv_ref, x_ref, wg_ref, wu_ref, wd_ref, o_ref,
                    xb_ref, acc_ref, *, nf):
    t = pl.program_id(0)
    f = pl.program_id(1)
    valid = tv_ref[t] > 0

    @pl.when(valid & (f == 0))
    def _():
        xb_ref[...] = x_ref[...].astype(BF16)
        acc_ref[...] = jnp.zeros_like(acc_ref)

    @pl.when(valid)
    def _():
        xb = xb_ref[...]
        gate = jnp.dot(xb, wg_ref[...], preferred_element_type=F32)
        up = jnp.dot(xb, wu_ref[...], preferred_element_type=F32)
        a = (gate * jax.nn.sigmoid(gate) * up).astype(BF16)
        acc_ref[...] += jnp.dot(a, wd_ref[...], preferred_element_type=F32)

    @pl.when(valid & (f == nf - 1))
    def _():
        o_ref[...] = acc_ref[...]

    @pl.when(jnp.logical_not(valid) & (f == nf - 1))
    def _():
        o_ref[...] = jnp.zeros_like(o_ref)


def _experts(tile_expert, tile_block, tile_valid, xs, w_gate, w_up, w_down, tm):
    rows, d = xs.shape
    ff = w_gate.shape[2]
    tf = _tile(ff, 512)
    nf = ff // tf
    n_tiles = tile_expert.shape[0]

    def fidx(t, f, tv):
        return jnp.where(tv[t] > 0, f, nf - 1)

    return pl.pallas_call(
        functools.partial(_experts_kernel, nf=nf),
        grid_spec=pltpu.PrefetchScalarGridSpec(
            num_scalar_prefetch=3,
            grid=(n_tiles, nf),
            in_specs=[
                pl.BlockSpec((tm, d), lambda t, f, te, tb, tv: (tb[t], 0)),
                pl.BlockSpec((None, d, tf), lambda t, f, te, tb, tv: (te[t], 0, fidx(t, f, tv))),
                pl.BlockSpec((None, d, tf), lambda t, f, te, tb, tv: (te[t], 0, fidx(t, f, tv))),
                pl.BlockSpec((None, tf, d), lambda t, f, te, tb, tv: (te[t], fidx(t, f, tv), 0)),
            ],
            out_specs=pl.BlockSpec((tm, d), lambda t, f, te, tb, tv: (t, 0)),
            scratch_shapes=[pltpu.VMEM((tm, d), BF16), pltpu.VMEM((tm, d), F32)],
        ),
        out_shape=jax.ShapeDtypeStruct((rows, d), F32),
        compiler_params=_params(("arbitrary", "arbitrary")),
        name="moe_experts",
    )(tile_expert, tile_block, tile_valid, xs, w_gate, w_up, w_down)


def _combine_kernel(pos_ref, x_ref, info_ref, gfin_ref, ys_hbm, o_ref, y1_ref, y2_ref, sem,
                    *, tc, final_norm):
    def row_copy(src_row, dst_ref, r):
        return pltpu.make_async_copy(ys_hbm.at[pl.ds(src_row, 1), :],
                                     dst_ref.at[pl.ds(r, 1), :], sem)

    def issue(r, carry):
        row_copy(pos_ref[0, r], y1_ref, r).start()
        row_copy(pos_ref[0, tc + r], y2_ref, r).start()
        return carry

    lax.fori_loop(0, tc, issue, 0)

    def drain(r, carry):
        row_copy(0, y1_ref, 0).wait()
        row_copy(0, y2_ref, 0).wait()
        return carry

    lax.fori_loop(0, tc, drain, 0)

    info = info_ref[...]
    w1 = info[:, _R_W1:_R_W1 + 1]
    w2 = info[:, _R_W2:_R_W2 + 1]
    out = x_ref[...] + (w1 * y1_ref[...] + w2 * y2_ref[...])
    if final_norm:
        var = jnp.mean(out * out, axis=-1, keepdims=True)
        out = out * lax.rsqrt(var + RMS_EPS) * gfin_ref[...]
    o_ref[...] = out


def _combine(pos_blocks, x2, info, g_final, ys, final_norm):
    n, d = x2.shape
    nblk, _, two_tc = pos_blocks.shape
    tc = two_tc // 2
    kern = functools.partial(_combine_kernel, tc=tc, final_norm=final_norm)
    return pl.pallas_call(
        kern,
        grid=(nblk,),
        in_specs=[
            pl.BlockSpec((None, 1, two_tc), lambda i: (i, 0, 0), memory_space=pltpu.SMEM),
            pl.BlockSpec((tc, d), lambda i: (i, 0)),
            pl.BlockSpec((tc, LANES), lambda i: (i, 0)),
            pl.BlockSpec((1, d), lambda i: (0, 0)),
            pl.BlockSpec(memory_space=pl.ANY),
        ],
        out_specs=pl.BlockSpec((tc, d), lambda i: (i, 0)),
        out_shape=jax.ShapeDtypeStruct((n, d), F32),
        scratch_shapes=[pltpu.VMEM((tc, d), F32), pltpu.VMEM((tc, d), F32),
                        pltpu.SemaphoreType.DMA(())],
        compiler_params=_params(("arbitrary",)),
        name="moe_combine",
    )(pos_blocks, x2, info, g_final, ys)


def _final_norm_kernel(x_ref, g_ref, o_ref):
    x = x_ref[...]
    var = jnp.mean(x * x, axis=-1, keepdims=True)
    o_ref[...] = x * lax.rsqrt(var + RMS_EPS) * g_ref[...]


def _final_norm(x2, g):
    n, d = x2.shape
    tm = _tile(n, 1024)
    return pl.pallas_call(
        _final_norm_kernel,
        grid=(n // tm,),
        in_specs=[pl.BlockSpec((tm, d), lambda i: (i, 0)), pl.BlockSpec((1, d), lambda i: (0, 0))],
        out_specs=pl.BlockSpec((tm, d), lambda i: (i, 0)),
        out_shape=jax.ShapeDtypeStruct((n, d), F32),
        compiler_params=_params(("arbitrary",)),
        name="final_norm",
    )(x2, g)


def _pos_blocks(pos1, pos2, tb):
    n = pos1.shape[0]
    both = jnp.stack([pos1.reshape(n // tb, tb), pos2.reshape(n // tb, tb)], axis=1)
    return both.reshape(n // tb, 1, 2 * tb)


def _moe(x2, g, w_router, w_gate, w_up, w_down, g_final, final_norm):
    n, d = x2.shape
    tm = _tile(n, 512)
    n_tiles = (2 * n) // tm + N_EXPERTS
    w_pad = jnp.zeros((d, LANES), F32).at[:, :N_EXPERTS].set(w_router)
    w_hi = w_pad.astype(BF16)
    w_lo = (w_pad - w_hi.astype(F32)).astype(BF16)
    h2, info, cnt = _router(x2, g, w_hi, w_lo)

    counts = cnt[0, :N_EXPERTS].astype(I32)
    tiles_per = (counts + tm - 1) // tm
    tile_ends = jnp.cumsum(tiles_per)
    used = tile_ends[-1]
    offs = (tile_ends - tiles_per) * tm
    e1 = info[:, _R_E1].astype(I32)
    e2 = info[:, _R_E2].astype(I32)
    pos1 = offs[e1] + info[:, _R_RANK1].astype(I32)
    pos2 = offs[e2] + info[:, _R_RANK2].astype(I32)
    fill = jnp.concatenate([jnp.stack([offs + counts, tile_ends * tm], axis=1).reshape(-1),
                            used[None]]).astype(I32)

    tb = _tile(n, 2048)
    xs = _dispatch(fill, _pos_blocks(pos1, pos2, tb), h2, n_tiles * tm, tm)

    t_idx = jnp.arange(n_tiles, dtype=I32)
    t_in = jnp.minimum(t_idx, used - 1)
    tile_expert = jnp.sum((t_in[:, None] >= tile_ends[None, :]).astype(I32), axis=1)
    tile_valid = (t_idx < used).astype(I32)
    ys = _experts(tile_expert, t_in, tile_valid, xs, w_gate, w_up, w_down, tm)

    tc = _tile(n, 512)
    return _combine(_pos_blocks(pos1, pos2, tc), x2, info, g_final, ys, final_norm)


def _rope_tables(seq):
    pos = jnp.arange(seq, dtype=F32)
    inv = 1.0 / (ROPE_THETA ** (jnp.arange(0, HEAD_DIM, 2, dtype=F32) / HEAD_DIM))
    ang = pos[:, None] * inv[None, :]
    ang = jnp.concatenate([ang, ang], axis=-1)
    cos = jnp.cos(ang)
    sin = jnp.sin(ang)
    sign = jnp.concatenate([-jnp.ones((HEAD_DIM // 2,), F32), jnp.ones((HEAD_DIM // 2,), F32)])
    reps = LANES // HEAD_DIM
    return jnp.tile(cos, (1, reps)), jnp.tile(sin * sign, (1, reps))


def kernel(x, norm_mix, w_in, conv_w, lam_q1, lam_k1, lam_q2, lam_k2, subln_g, w_out, norm_ffn,
           ffn_w_gate, ffn_w_up, ffn_w_down, router_w, exp_w_gate, exp_w_up, exp_w_down,
           norm_final):
    batch, seq, d = x.shape
    depth = w_in.shape[0]
    n_heads = d // V_HEAD_DIM
    n = batch * seq
    assert w_in.shape[2] == 8 * d and seq % 8 == 0 and d % LANES == 0
    cos_t, sin_t = _rope_tables(seq)
    x2 = x.reshape(n, d)
    row = lambda v: v.reshape(1, -1)

    for l in range(depth):
        lam_init = 0.8 - 0.6 * math.exp(-0.3 * l)
        w_in_l = w_in[l].astype(BF16)
        g_mix = row(norm_mix[l])
        qkv = _qkv_proj(x2, g_mix, w_in_l, cos_t, sin_t, seq)
        zc, sga = _conv_branch(x2, g_mix, w_in_l, conv_w[l], seq)
        y_attn = _attention(qkv, row(lam_q1[l]), row(lam_k1[l]), row(lam_q2[l]), row(lam_k2[l]),
                            row(subln_g[l]), batch, seq, n_heads, lam_init)
        x2 = _out_proj(x2, y_attn, sga, zc, w_out[l].astype(BF16))
        g_ffn = row(norm_ffn[l])
        j = l // 2
        last = l == depth - 1
        if l % 2 == 0:
            x2 = _ffn_dense(x2, g_ffn, ffn_w_gate[j].astype(BF16), ffn_w_up[j].astype(BF16),
                            ffn_w_down[j].astype(BF16))
            if last:
                x2 = _final_norm(x2, row(norm_final))
        else:
            x2 = _moe(x2, g_ffn, router_w[j], exp_w_gate[j].astype(BF16),
                      exp_w_up[j].astype(BF16), exp_w_down[j].astype(BF16),
                      row(norm_final), last)
    return x2.reshape(batch, seq, d)
```

```python
import functools
import math

import jax
import jax.numpy as jnp
from jax import lax
from jax.experimental import pallas as pl
from jax.experimental.pallas import tpu as pltpu

F32 = jnp.float32
BF16 = jnp.bfloat16
I32 = jnp.int32

HEAD_DIM = 64
V_HEAD_DIM = 2 * HEAD_DIM
N_EXPERTS = 8
CONV_K = 3
ROPE_THETA = 10000.0
RMS_EPS = 1e-5
LANES = 128
MIB = 1024 * 1024
VMEM_LIMIT = 56 * MIB


def _tile(dim, pref):
    if dim <= pref:
        return dim
    t = pref
    while t >= LANES:
        if dim % t == 0:
            return t
        t -= LANES
    return dim


def _params(sem):
    return pltpu.CompilerParams(dimension_semantics=sem, vmem_limit_bytes=VMEM_LIMIT)


def _dot_w(a, w_ref):
    return jnp.dot(a, w_ref[...].astype(BF16), preferred_element_type=F32)


def _rmsnorm_bf16(x, g):
    var = jnp.mean(x * x, axis=-1, keepdims=True)
    return (x * lax.rsqrt(var + RMS_EPS) * g).astype(BF16)


Q_SCALE = HEAD_DIM ** -0.5 * math.log2(math.e)
MXU_COLS = 256


def _head_lane_perm(n_heads):
    c = jnp.arange(LANES)
    half, m, idx = c // HEAD_DIM, (c // (HEAD_DIM // 2)) % 2, c % (HEAD_DIM // 2)
    within = m * HEAD_DIM + half * (HEAD_DIM // 2) + idx
    return (jnp.arange(n_heads)[:, None] * LANES + within[None, :]).reshape(-1)


def _qk_kernel(x_ref, g_ref, w_ref, cos_ref, sin_ref, o_ref, h_ref, *, n_q):
    j = pl.program_id(1)

    @pl.when(j == 0)
    def _():
        h_ref[...] = _rmsnorm_bf16(x_ref[...], g_ref[...])

    scale = jnp.where(j < n_q, Q_SCALE, 1.0).astype(F32)
    cos = cos_ref[...] * scale
    sin = sin_ref[...] * scale
    h = h_ref[...]
    tn = w_ref.shape[1]
    for c in range(tn // MXU_COLS):
        y = jnp.dot(h, w_ref[:, c * MXU_COLS:(c + 1) * MXU_COLS], preferred_element_type=F32)
        for s in range(MXU_COLS // LANES):
            ys = y[:, s * LANES:(s + 1) * LANES]
            lo = c * MXU_COLS + s * LANES
            o_ref[:, lo:lo + LANES] = (ys * cos + pltpu.roll(ys, LANES // 2, 1) * sin).astype(BF16)


def _qk_proj(x2, g, w_qk, cos_t, sin_t, seq):
    n, d = x2.shape
    width = w_qk.shape[1]
    tm = _tile(seq, 1024)
    tn = _tile(d, 512)
    kern = functools.partial(_qk_kernel, n_q=d // tn)
    t_blocks = seq // tm
    return pl.pallas_call(
        kern,
        grid=(n // tm, width // tn),
        in_specs=[
            pl.BlockSpec((tm, d), lambda i, j: (i, 0)),
            pl.BlockSpec((1, d), lambda i, j: (0, 0)),
            pl.BlockSpec((d, tn), lambda i, j: (0, j)),
            pl.BlockSpec((tm, LANES), lambda i, j: (i % t_blocks, 0)),
            pl.BlockSpec((tm, LANES), lambda i, j: (i % t_blocks, 0)),
        ],
        out_specs=pl.BlockSpec((tm, tn), lambda i, j: (i, j)),
        out_shape=jax.ShapeDtypeStruct((n, width), BF16),
        scratch_shapes=[pltpu.VMEM((tm, d), BF16)],
        compiler_params=_params(("arbitrary", "arbitrary")),
        name="qk_rope",
    )(x2, g, w_qk, cos_t, sin_t)


def _conv_kernel(x_ref, g_ref, wv_ref, wb_ref, wc_ref, wu_ref, wga_ref, wgc_ref, cw_ref,
                 v_ref, zc_ref, sga_ref, h_ref):
    c = pl.program_id(1)

    @pl.when(c == 0)
    def _():
        h_ref[...] = _rmsnorm_bf16(x_ref[...], g_ref[...])

    h = h_ref[...]
    v_ref[...] = _dot_w(h, wv_ref).astype(BF16)
    cu = _dot_w(h, wc_ref) * _dot_w(h, wu_ref)
    row = lax.broadcasted_iota(I32, cu.shape, 0)
    cu1 = jnp.where(row >= 1, pltpu.roll(cu, 1, 0), 0.0)
    cu2 = jnp.where(row >= 2, pltpu.roll(cu, 2, 0), 0.0)
    cw = cw_ref[...]
    y = cw[0:1, :] * cu2
    y = y + cw[1:2, :] * cu1
    y = y + cw[2:3, :] * cu
    y_conv = _dot_w(h, wb_ref) * y
    zc_ref[...] = (jax.nn.sigmoid(_dot_w(h, wgc_ref)) * y_conv).astype(BF16)
    sga_ref[...] = jax.nn.sigmoid(_dot_w(h, wga_ref)).astype(BF16)


def _conv_branch(x2, g, w_in, conv_w, layer, seq):
    n, d = x2.shape
    tn = _tile(d, 256)
    nc = d // tn

    def wspec(k):
        return pl.BlockSpec((None, d, tn), lambda b, c: (layer, 0, k * nc + c))

    out = jax.ShapeDtypeStruct((n, d), BF16)
    ospec = pl.BlockSpec((seq, tn), lambda b, c: (b, c))
    return pl.pallas_call(
        _conv_kernel,
        grid=(n // seq, nc),
        in_specs=[
            pl.BlockSpec((seq, d), lambda b, c: (b, 0)),
            pl.BlockSpec((1, d), lambda b, c: (0, 0)),
            wspec(2), wspec(3), wspec(4), wspec(5), wspec(6), wspec(7),
            pl.BlockSpec((None, CONV_K, tn), lambda b, c: (layer, 0, c)),
        ],
        out_specs=[ospec, ospec, ospec],
        out_shape=[out, out, out],
        scratch_shapes=[pltpu.VMEM((seq, d), BF16)],
        compiler_params=_params(("arbitrary", "arbitrary")),
        name="v_conv_gate",
    )(x2, g, w_in, w_in, w_in, w_in, w_in, w_in, conv_w)


def _attn_kernel(lq1_ref, lk1_ref, lq2_ref, lk2_ref, q_ref, k_ref, v_ref, g_ref, o_ref,
                 *, tq, lam_init):
    qi = pl.program_id(2)
    lam = (jnp.exp(jnp.sum(lq1_ref[...] * lk1_ref[...], axis=-1, keepdims=True))
           - jnp.exp(jnp.sum(lq2_ref[...] * lk2_ref[...], axis=-1, keepdims=True))
           + lam_init)

    q = q_ref[...]
    lane = lax.broadcasted_iota(I32, q.shape, 1)
    zero = jnp.zeros_like(q)
    map0 = (lane & (HEAD_DIM // 2)) == 0
    q_maps = (jnp.where(map0, q, zero), jnp.where(map0, zero, q))
    nt = (((1,), (1,)), ((), ()))

    def scores(j):
        k = k_ref[pl.ds(pl.multiple_of(j * tq, tq), tq), :]
        return [lax.dot_general(qm, k, nt, preferred_element_type=F32) for qm in q_maps]

    def update(state, s, v):
        m, l, acc = state
        m_new = jnp.maximum(m, jnp.max(s, axis=-1, keepdims=True))
        alpha = jnp.exp2(m - m_new)
        p = jnp.exp2(s - m_new)
        l = alpha * l + jnp.sum(p, axis=-1, keepdims=True)
        acc = alpha * acc + jnp.dot(p.astype(BF16), v, preferred_element_type=F32)
        return m_new, l, acc

    def body(j, states):
        v = v_ref[pl.ds(pl.multiple_of(j * tq, tq), tq), :]
        return tuple(update(st, s, v) for st, s in zip(states, scores(j)))

    init = (jnp.full((tq, 1), -jnp.inf, F32), jnp.zeros((tq, 1), F32),
            jnp.zeros((tq, V_HEAD_DIM), F32))
    states = lax.fori_loop(0, qi, body, (init, init))

    row = lax.broadcasted_iota(I32, (tq, tq), 0)
    col = lax.broadcasted_iota(I32, (tq, tq), 1)
    neg = jnp.finfo(F32).min
    v = v_ref[pl.ds(pl.multiple_of(qi * tq, tq), tq), :]
    states = tuple(update(st, jnp.where(col <= row, s, neg), v)
                   for st, s in zip(states, scores(qi)))

    (_, l1, a1), (_, l2, a2) = states
    o = a1 / l1 - lam * (a2 / l2)
    var = jnp.mean(o * o, axis=-1, keepdims=True)
    y = (o * lax.rsqrt(var + RMS_EPS) * g_ref[...]) * (1.0 - lam_init)
    o_ref[...] = y.astype(BF16)


def _attention(qk, v, lq1, lk1, lq2, lk2, subln_g, batch, seq, n_heads, lam_init):
    n = qk.shape[0]
    tq = _tile(seq, 512)
    nq = seq // tq
    kern = functools.partial(_attn_kernel, tq=tq, lam_init=lam_init)
    lspec = pl.BlockSpec((1, HEAD_DIM), lambda b, h, i: (0, 0))
    return pl.pallas_call(
        kern,
        grid=(batch, n_heads, nq),
        in_specs=[
            lspec, lspec, lspec, lspec,
            pl.BlockSpec((tq, V_HEAD_DIM), lambda b, h, i: (b * nq + i, h)),
            pl.BlockSpec((seq, V_HEAD_DIM), lambda b, h, i: (b, n_heads + h)),
            pl.BlockSpec((seq, V_HEAD_DIM), lambda b, h, i: (b, h)),
            pl.BlockSpec((1, V_HEAD_DIM), lambda b, h, i: (0, 0)),
        ],
        out_specs=pl.BlockSpec((tq, V_HEAD_DIM), lambda b, h, i: (b * nq + i, h)),
        out_shape=jax.ShapeDtypeStruct((n, n_heads * V_HEAD_DIM), BF16),
        compiler_params=_params(("arbitrary", "arbitrary", "arbitrary")),
        name="diff_attention",
    )(lq1, lk1, lq2, lk2, qk, qk, v, subln_g)


def _outproj_kernel(x_ref, ya_ref, sga_ref, zc_ref, w_ref, o_ref):
    m = sga_ref[...].astype(F32) * ya_ref[...].astype(F32) + zc_ref[...].astype(F32)
    o_ref[...] = x_ref[...] + _dot_w(m.astype(BF16), w_ref)


def _out_proj(x2, y_attn, sga, zc, w_out, layer):
    n, d = x2.shape
    tm = _tile(n, 512)
    row = pl.BlockSpec((tm, d), lambda i: (i, 0))
    return pl.pallas_call(
        _outproj_kernel,
        grid=(n // tm,),
        in_specs=[row, row, row, row, pl.BlockSpec((None, d, d), lambda i: (layer, 0, 0))],
        out_specs=row,
        out_shape=jax.ShapeDtypeStruct((n, d), F32),
        compiler_params=_params(("arbitrary",)),
        name="out_proj",
    )(x2, y_attn, sga, zc, w_out)


def _ffn_kernel(x_ref, g_ref, wg_ref, wu_ref, wd_ref, o_ref, h_ref, acc_ref, *, nf):
    f = pl.program_id(1)

    @pl.when(f == 0)
    def _():
        h_ref[...] = _rmsnorm_bf16(x_ref[...], g_ref[...])
        acc_ref[...] = jnp.zeros_like(acc_ref)

    h = h_ref[...]
    gate = _dot_w(h, wg_ref)
    up = _dot_w(h, wu_ref)
    a = (gate * jax.nn.sigmoid(gate) * up).astype(BF16)
    acc_ref[...] += _dot_w(a, wd_ref)

    @pl.when(f == nf - 1)
    def _():
        o_ref[...] = x_ref[...] + acc_ref[...]


def _ffn_dense(x2, g, w_gate, w_up, w_down, layer):
    n, d = x2.shape
    ff = w_gate.shape[2]
    tm = _tile(n, 1024)
    tf = _tile(ff, 256)
    nf = ff // tf
    return pl.pallas_call(
        functools.partial(_ffn_kernel, nf=nf),
        grid=(n // tm, nf),
        in_specs=[
            pl.BlockSpec((tm, d), lambda i, f: (i, 0)),
            pl.BlockSpec((1, d), lambda i, f: (0, 0)),
            pl.BlockSpec((None, d, tf), lambda i, f: (layer, 0, f)),
            pl.BlockSpec((None, d, tf), lambda i, f: (layer, 0, f)),
            pl.BlockSpec((None, tf, d), lambda i, f: (layer, f, 0)),
        ],
        out_specs=pl.BlockSpec((tm, d), lambda i, f: (i, 0)),
        out_shape=jax.ShapeDtypeStruct((n, d), F32),
        scratch_shapes=[pltpu.VMEM((tm, d), BF16), pltpu.VMEM((tm, d), F32)],
        compiler_params=_params(("arbitrary", "arbitrary")),
        name="ffn_dense",
    )(x2, g, w_gate, w_up, w_down)


_R_E1, _R_E2, _R_RANK1, _R_RANK2, _R_W1, _R_W2 = range(6)


def _router_kernel(x_ref, g_ref, whi_ref, wlo_ref, h_ref, info_ref, cnt_ref, tri_ref, carry_ref):
    i = pl.program_id(0)
    tm = x_ref.shape[0]

    @pl.when(i == 0)
    def _():
        r = lax.broadcasted_iota(I32, (tm, tm), 0)
        c = lax.broadcasted_iota(I32, (tm, tm), 1)
        tri_ref[...] = jnp.where(c < r, 1.0, 0.0).astype(BF16)
        carry_ref[...] = jnp.zeros_like(carry_ref)

    x = x_ref[...]
    var = jnp.mean(x * x, axis=-1, keepdims=True)
    h = x * lax.rsqrt(var + RMS_EPS) * g_ref[...]
    h_ref[...] = h

    h_hi = h.astype(BF16)
    h_lo = (h - h_hi.astype(F32)).astype(BF16)
    dot = functools.partial(jnp.dot, preferred_element_type=F32)
    logits = dot(h_hi, whi_ref[...]) + dot(h_lo, whi_ref[...]) + dot(h_hi, wlo_ref[...])

    lane = lax.broadcasted_iota(I32, (tm, LANES), 1)
    ninf = -jnp.inf
    lg = jnp.where(lane < N_EXPERTS, logits, ninf)
    m1 = jnp.max(lg, axis=-1, keepdims=True)
    e1 = jnp.min(jnp.where(lg == m1, lane, LANES), axis=-1, keepdims=True)
    lg2 = jnp.where(lane == e1, ninf, lg)
    m2 = jnp.max(lg2, axis=-1, keepdims=True)
    e2 = jnp.min(jnp.where(lg2 == m2, lane, LANES), axis=-1, keepdims=True)
    ex2 = jnp.exp(m2 - m1)
    den = 1.0 + ex2
    w1 = 1.0 / den
    w2 = ex2 / den

    oh1 = lane == e1
    oh2 = lane == e2
    oh = jnp.where(oh1 | oh2, 1.0, 0.0)
    carry = carry_ref[0:1, :]
    before = dot(tri_ref[...], oh.astype(BF16)) + carry
    rank1 = jnp.sum(jnp.where(oh1, before, 0.0), axis=-1, keepdims=True)
    rank2 = jnp.sum(jnp.where(oh2, before, 0.0), axis=-1, keepdims=True)
    new_carry = carry + jnp.sum(oh, axis=0, keepdims=True)
    carry_ref[...] = jnp.broadcast_to(new_carry, carry_ref.shape)
    cnt_ref[...] = jnp.broadcast_to(new_carry, cnt_ref.shape)

    info = jnp.zeros((tm, LANES), F32)
    for k, val in ((_R_E1, e1.astype(F32)), (_R_E2, e2.astype(F32)), (_R_RANK1, rank1),
                   (_R_RANK2, rank2), (_R_W1, w1), (_R_W2, w2)):
        info = jnp.where(lane == k, val, info)
    info_ref[...] = info


def _router(x2, g, w_hi, w_lo):
    n, d = x2.shape
    tm = _tile(n, 1024)
    return pl.pallas_call(
        _router_kernel,
        grid=(n // tm,),
        in_specs=[
            pl.BlockSpec((tm, d), lambda i: (i, 0)),
            pl.BlockSpec((1, d), lambda i: (0, 0)),
            pl.BlockSpec((d, LANES), lambda i: (0, 0)),
            pl.BlockSpec((d, LANES), lambda i: (0, 0)),
        ],
        out_specs=[
            pl.BlockSpec((tm, d), lambda i: (i, 0)),
            pl.BlockSpec((tm, LANES), lambda i: (i, 0)),
            pl.BlockSpec((8, LANES), lambda i: (0, 0)),
        ],
        out_shape=[
            jax.ShapeDtypeStruct((n, d), F32),
            jax.ShapeDtypeStruct((n, LANES), F32),
            jax.ShapeDtypeStruct((8, LANES), F32),
        ],
        scratch_shapes=[pltpu.VMEM((tm, tm), BF16), pltpu.VMEM((8, LANES), F32)],
        compiler_params=_params(("arbitrary",)),
        name="moe_router",
    )(x2, g, w_hi, w_lo)


def _dispatch_kernel(fill_ref, pos_ref, h_ref, xs_hbm, zero_ref, sem, *, tb, tm, n_steps, rows):
    i = pl.program_id(0)

    def row_copy(src_row, dst_row):
        return pltpu.make_async_copy(h_ref.at[pl.ds(src_row, 1), :],
                                     xs_hbm.at[pl.ds(dst_row, 1), :], sem)

    def issue(r, carry):
        row_copy(r, pos_ref[0, r]).start()
        row_copy(r, pos_ref[0, tb + r]).start()
        return carry

    lax.fori_loop(0, tb, issue, 0)

    def drain(r, carry):
        row_copy(0, 0).wait()
        row_copy(0, 0).wait()
        return carry

    lax.fori_loop(0, tb, drain, 0)

    @pl.when(i == n_steps - 1)
    def _():
        zero_ref[...] = jnp.zeros_like(zero_ref)

        def zero_row(dst_row):
            return pltpu.make_async_copy(zero_ref.at[pl.ds(0, 1), :],
                                         xs_hbm.at[pl.ds(dst_row, 1), :], sem)

        def zero_tile(t):
            return pltpu.make_async_copy(
                zero_ref, xs_hbm.at[pl.ds(pl.multiple_of(t * tm, tm), tm), :], sem)

        for e in range(N_EXPERTS):
            lo = fill_ref[2 * e]
            hi = fill_ref[2 * e + 1]
            lax.fori_loop(lo, hi, lambda r, c: (zero_row(r).start(), c)[1], 0)
            lax.fori_loop(lo, hi, lambda r, c: (zero_row(0).wait(), c)[1], 0)
        first_unused = fill_ref[2 * N_EXPERTS]
        lax.fori_loop(first_unused, rows // tm, lambda t, c: (zero_tile(t).start(), c)[1], 0)
        lax.fori_loop(first_unused, rows // tm, lambda t, c: (zero_tile(0).wait(), c)[1], 0)


def _dispatch(fill, pos_blocks, h2, rows, tm):
    n, d = h2.shape
    nblk, _, two_tb = pos_blocks.shape
    tb = two_tb // 2
    kern = functools.partial(_dispatch_kernel, tb=tb, tm=tm, n_steps=nblk, rows=rows)
    return pl.pallas_call(
        kern,
        grid_spec=pltpu.PrefetchScalarGridSpec(
            num_scalar_prefetch=1,
            grid=(nblk,),
            in_specs=[
                pl.BlockSpec((None, 1, two_tb), lambda i, fill: (i, 0, 0), memory_space=pltpu.SMEM),
                pl.BlockSpec((tb, d), lambda i, fill: (i, 0)),
            ],
            out_specs=pl.BlockSpec(memory_space=pl.ANY),
            scratch_shapes=[pltpu.VMEM((tm, d), F32), pltpu.SemaphoreType.DMA(())],
        ),
        out_shape=jax.ShapeDtypeStruct((rows, d), F32),
        compiler_params=_params(("arbitrary",)),
        name="moe_dispatch",
    )(fill, pos_blocks, h2)


def _experts_kernel(te_ref, tblk_ref, tv_ref, x_ref, wg_ref, wu_ref, wd_ref, o_ref,
                    xb_ref, acc_ref, *, nf, sub):
    t = pl.program_id(0)
    f = pl.program_id(1)
    n_rows = tv_ref[t]
    valid = n_rows > 0

    @pl.when(valid & (f == 0))
    def _():
        xb_ref[...] = x_ref[...].astype(BF16)
        acc_ref[...] = jnp.zeros_like(acc_ref)

    for s in range(x_ref.shape[0] // sub):
        @pl.when(n_rows > s * sub)
        def _():
            rows = slice(s * sub, (s + 1) * sub)
            xb = xb_ref[rows, :]
            gate = _dot_w(xb, wg_ref)
            up = _dot_w(xb, wu_ref)
            a = (gate * jax.nn.sigmoid(gate) * up).astype(BF16)
            acc_ref[rows, :] += _dot_w(a, wd_ref)

    @pl.when(valid & (f == nf - 1))
    def _():
        o_ref[...] = acc_ref[...]

    @pl.when(jnp.logical_not(valid) & (f == nf - 1))
    def _():
        o_ref[...] = jnp.zeros_like(o_ref)


def _experts(tile_expert, tile_block, tile_valid, xs, w_gate, w_up, w_down, layer, tm):
    rows, d = xs.shape
    ff = w_gate.shape[3]
    tf = _tile(ff, 512)
    nf = ff // tf
    n_tiles = tile_expert.shape[0]

    def fidx(t, f, tv):
        return jnp.where(tv[t] > 0, f, nf - 1)

    return pl.pallas_call(
        functools.partial(_experts_kernel, nf=nf, sub=_tile(tm, 512)),
        grid_spec=pltpu.PrefetchScalarGridSpec(
            num_scalar_prefetch=3,
            grid=(n_tiles, nf),
            in_specs=[
                pl.BlockSpec((tm, d), lambda t, f, te, tb, tv: (tb[t], 0)),
                pl.BlockSpec((None, None, d, tf),
                             lambda t, f, te, tb, tv: (layer, te[t], 0, fidx(t, f, tv))),
                pl.BlockSpec((None, None, d, tf),
                             lambda t, f, te, tb, tv: (layer, te[t], 0, fidx(t, f, tv))),
                pl.BlockSpec((None, None, tf, d),
                             lambda t, f, te, tb, tv: (layer, te[t], fidx(t, f, tv), 0)),
            ],
            out_specs=pl.BlockSpec((tm, d), lambda t, f, te, tb, tv: (t, 0)),
            scratch_shapes=[pltpu.VMEM((tm, d), BF16), pltpu.VMEM((tm, d), F32)],
        ),
        out_shape=jax.ShapeDtypeStruct((rows, d), F32),
        compiler_params=_params(("arbitrary", "arbitrary")),
        name="moe_experts",
    )(tile_expert, tile_block, tile_valid, xs, w_gate, w_up, w_down)


def _combine_kernel(pos_ref, x_ref, info_ref, gfin_ref, ys_hbm, o_ref, y1_ref, y2_ref, sem,
                    *, tc, final_norm):
    def row_copy(src_row, dst_ref, r):
        return pltpu.make_async_copy(ys_hbm.at[pl.ds(src_row, 1), :],
                                     dst_ref.at[pl.ds(r, 1), :], sem)

    def issue(r, carry):
        row_copy(pos_ref[0, r], y1_ref, r).start()
        row_copy(pos_ref[0, tc + r], y2_ref, r).start()
        return carry

    lax.fori_loop(0, tc, issue, 0)

    def drain(r, carry):
        row_copy(0, y1_ref, 0).wait()
        row_copy(0, y2_ref, 0).wait()
        return carry

    lax.fori_loop(0, tc, drain, 0)

    info = info_ref[...]
    w1 = info[:, _R_W1:_R_W1 + 1]
    w2 = info[:, _R_W2:_R_W2 + 1]
    out = x_ref[...] + (w1 * y1_ref[...] + w2 * y2_ref[...])
    if final_norm:
        var = jnp.mean(out * out, axis=-1, keepdims=True)
        out = out * lax.rsqrt(var + RMS_EPS) * gfin_ref[...]
    o_ref[...] = out


def _combine(pos_blocks, x2, info, g_final, ys, final_norm):
    n, d = x2.shape
    nblk, _, two_tc = pos_blocks.shape
    tc = two_tc // 2
    kern = functools.partial(_combine_kernel, tc=tc, final_norm=final_norm)
    return pl.pallas_call(
        kern,
        grid=(nblk,),
        in_specs=[
            pl.BlockSpec((None, 1, two_tc), lambda i: (i, 0, 0), memory_space=pltpu.SMEM),
            pl.BlockSpec((tc, d), lambda i: (i, 0)),
            pl.BlockSpec((tc, LANES), lambda i: (i, 0)),
            pl.BlockSpec((1, d), lambda i: (0, 0)),
            pl.BlockSpec(memory_space=pl.ANY),
        ],
        out_specs=pl.BlockSpec((tc, d), lambda i: (i, 0)),
        out_shape=jax.ShapeDtypeStruct((n, d), F32),
        scratch_shapes=[pltpu.VMEM((tc, d), F32), pltpu.VMEM((tc, d), F32),
                        pltpu.SemaphoreType.DMA(())],
        compiler_params=_params(("arbitrary",)),
        name="moe_combine",
    )(pos_blocks, x2, info, g_final, ys)


def _final_norm_kernel(x_ref, g_ref, o_ref):
    x = x_ref[...]
    var = jnp.mean(x * x, axis=-1, keepdims=True)
    o_ref[...] = x * lax.rsqrt(var + RMS_EPS) * g_ref[...]


def _final_norm(x2, g):
    n, d = x2.shape
    tm = _tile(n, 1024)
    return pl.pallas_call(
        _final_norm_kernel,
        grid=(n // tm,),
        in_specs=[pl.BlockSpec((tm, d), lambda i: (i, 0)), pl.BlockSpec((1, d), lambda i: (0, 0))],
        out_specs=pl.BlockSpec((tm, d), lambda i: (i, 0)),
        out_shape=jax.ShapeDtypeStruct((n, d), F32),
        compiler_params=_params(("arbitrary",)),
        name="final_norm",
    )(x2, g)


def _pos_blocks(pos1, pos2, tb):
    n = pos1.shape[0]
    both = jnp.stack([pos1.reshape(n // tb, tb), pos2.reshape(n // tb, tb)], axis=1)
    return both.reshape(n // tb, 1, 2 * tb)


def _moe(x2, g, w_router, w_gate, w_up, w_down, layer, g_final, final_norm):
    n, d = x2.shape
    tm = _tile(n, 1024)
    n_tiles = (2 * n) // tm + N_EXPERTS
    w_pad = jnp.zeros((d, LANES), F32).at[:, :N_EXPERTS].set(w_router)
    w_hi = w_pad.astype(BF16)
    w_lo = (w_pad - w_hi.astype(F32)).astype(BF16)
    h2, info, cnt = _router(x2, g, w_hi, w_lo)

    counts = cnt[0, :N_EXPERTS].astype(I32)
    tiles_per = (counts + tm - 1) // tm
    tile_ends = jnp.cumsum(tiles_per)
    used = tile_ends[-1]
    offs = (tile_ends - tiles_per) * tm
    e1 = info[:, _R_E1].astype(I32)
    e2 = info[:, _R_E2].astype(I32)
    pos1 = offs[e1] + info[:, _R_RANK1].astype(I32)
    pos2 = offs[e2] + info[:, _R_RANK2].astype(I32)
    fill = jnp.concatenate([jnp.stack([offs + counts, tile_ends * tm], axis=1).reshape(-1),
                            used[None]]).astype(I32)

    tb = _tile(n, 1024)
    xs = _dispatch(fill, _pos_blocks(pos1, pos2, tb), h2, n_tiles * tm, tm)

    t_idx = jnp.arange(n_tiles, dtype=I32)
    t_in = jnp.minimum(t_idx, used - 1)
    tile_expert = jnp.sum((t_in[:, None] >= tile_ends[None, :]).astype(I32), axis=1)
    rows_left = (offs + counts)[tile_expert] - t_in * tm
    tile_rows = jnp.where(t_idx < used, jnp.minimum(rows_left, tm), 0).astype(I32)
    ys = _experts(tile_expert, t_in, tile_rows, xs, w_gate, w_up, w_down, layer, tm)

    tc = _tile(n, 512)
    return _combine(_pos_blocks(pos1, pos2, tc), x2, info, g_final, ys, final_norm)


def _rope_tables(seq):
    pos = jnp.arange(seq, dtype=F32)
    inv = 1.0 / (ROPE_THETA ** (jnp.arange(0, HEAD_DIM, 2, dtype=F32) / HEAD_DIM))
    ang = pos[:, None] * inv[None, :]
    reps = LANES // (HEAD_DIM // 2)
    cos = jnp.tile(jnp.cos(ang), (1, reps))
    sin = jnp.tile(jnp.sin(ang), (1, reps))
    sign = jnp.where(jnp.arange(LANES) < LANES // 2, -1.0, 1.0).astype(F32)
    return cos, sin * sign


def kernel(x, norm_mix, w_in, conv_w, lam_q1, lam_k1, lam_q2, lam_k2, subln_g, w_out, norm_ffn,
           ffn_w_gate, ffn_w_up, ffn_w_down, router_w, exp_w_gate, exp_w_up, exp_w_down,
           norm_final):
    batch, seq, d = x.shape
    depth = w_in.shape[0]
    n_heads = d // V_HEAD_DIM
    n = batch * seq
    assert w_in.shape[2] == 8 * d and seq % 8 == 0 and d % LANES == 0
    cos_t, sin_t = _rope_tables(seq)
    qk_perm = _head_lane_perm(2 * n_heads)
    x2 = x.reshape(n, d)
    row = lambda v: v.reshape(1, -1)

    for l in range(depth):
        lam_init = 0.8 - 0.6 * math.exp(-0.3 * l)
        w_qk = jnp.take(w_in[l, :, :2 * d], qk_perm, axis=1).astype(BF16)
        g_mix = row(norm_mix[l])
        qk = _qk_proj(x2, g_mix, w_qk, cos_t, sin_t, seq)
        v, zc, sga = _conv_branch(x2, g_mix, w_in, conv_w, l, seq)
        y_attn = _attention(qk, v, row(lam_q1[l]), row(lam_k1[l]), row(lam_q2[l]), row(lam_k2[l]),
                            row(subln_g[l]), batch, seq, n_heads, lam_init)
        x2 = _out_proj(x2, y_attn, sga, zc, w_out, l)
        g_ffn = row(norm_ffn[l])
        j = l // 2
        last = l == depth - 1
        if l % 2 == 0:
            x2 = _ffn_dense(x2, g_ffn, ffn_w_gate, ffn_w_up, ffn_w_down, j)
            if last:
                x2 = _final_norm(x2, row(norm_final))
        else:
            x2 = _moe(x2, g_ffn, router_w[j], exp_w_gate, exp_w_up, exp_w_down, j,
                      row(norm_final), last)
    return x2.reshape(batch, seq, d)
```

```python
import functools
import math

import jax
import jax.numpy as jnp
from jax import lax
from jax.experimental import pallas as pl
from jax.experimental.pallas import tpu as pltpu

F32 = jnp.float32
BF16 = jnp.bfloat16
I32 = jnp.int32

HEAD_DIM = 64
V_HEAD_DIM = 2 * HEAD_DIM
N_EXPERTS = 8
CONV_K = 3
ROPE_THETA = 10000.0
RMS_EPS = 1e-5
LANES = 128
MIB = 1024 * 1024
VMEM_LIMIT = 56 * MIB


def _tile(dim, pref):
    if dim <= pref:
        return dim
    t = pref
    while t >= LANES:
        if dim % t == 0:
            return t
        t -= LANES
    return dim


def _params(sem):
    return pltpu.CompilerParams(dimension_semantics=sem, vmem_limit_bytes=VMEM_LIMIT)


def _dot_w(a, w_ref):
    return jnp.dot(a, w_ref[...].astype(BF16), preferred_element_type=F32)


def _rmsnorm_bf16(x, g):
    var = jnp.mean(x * x, axis=-1, keepdims=True)
    return (x * lax.rsqrt(var + RMS_EPS) * g).astype(BF16)


Q_SCALE = HEAD_DIM ** -0.5 * math.log2(math.e)
MXU_COLS = 256


def _head_lane_perm(n_heads):
    c = jnp.arange(LANES)
    half, m, idx = c // HEAD_DIM, (c // (HEAD_DIM // 2)) % 2, c % (HEAD_DIM // 2)
    within = m * HEAD_DIM + half * (HEAD_DIM // 2) + idx
    return (jnp.arange(n_heads)[:, None] * LANES + within[None, :]).reshape(-1)


def _qk_kernel(x_ref, g_ref, w_ref, cos_ref, sin_ref, o_ref, h_ref, *, n_q):
    j = pl.program_id(1)

    @pl.when(j == 0)
    def _():
        h_ref[...] = _rmsnorm_bf16(x_ref[...], g_ref[...])

    scale = jnp.where(j < n_q, Q_SCALE, 1.0).astype(F32)
    cos = cos_ref[...] * scale
    sin = sin_ref[...] * scale
    h = h_ref[...]
    tn = w_ref.shape[1]
    for c in range(tn // MXU_COLS):
        y = jnp.dot(h, w_ref[:, c * MXU_COLS:(c + 1) * MXU_COLS], preferred_element_type=F32)
        for s in range(MXU_COLS // LANES):
            ys = y[:, s * LANES:(s + 1) * LANES]
            lo = c * MXU_COLS + s * LANES
            o_ref[:, lo:lo + LANES] = (ys * cos + pltpu.roll(ys, LANES // 2, 1) * sin).astype(BF16)


def _qk_proj(x2, g, w_qk, cos_t, sin_t, seq):
    n, d = x2.shape
    width = w_qk.shape[1]
    tm = _tile(seq, 1024)
    tn = _tile(d, 512)
    kern = functools.partial(_qk_kernel, n_q=d // tn)
    t_blocks = seq // tm
    return pl.pallas_call(
        kern,
        grid=(n // tm, width // tn),
        in_specs=[
            pl.BlockSpec((tm, d), lambda i, j: (i, 0)),
            pl.BlockSpec((1, d), lambda i, j: (0, 0)),
            pl.BlockSpec((d, tn), lambda i, j: (0, j)),
            pl.BlockSpec((tm, LANES), lambda i, j: (i % t_blocks, 0)),
            pl.BlockSpec((tm, LANES), lambda i, j: (i % t_blocks, 0)),
        ],
        out_specs=pl.BlockSpec((tm, tn), lambda i, j: (i, j)),
        out_shape=jax.ShapeDtypeStruct((n, width), BF16),
        scratch_shapes=[pltpu.VMEM((tm, d), BF16)],
        compiler_params=_params(("arbitrary", "arbitrary")),
        name="qk_rope",
    )(x2, g, w_qk, cos_t, sin_t)


def _conv_kernel(x_ref, g_ref, wv_ref, wb_ref, wc_ref, wu_ref, wga_ref, wgc_ref, cw_ref,
                 v_ref, zc_ref, sga_ref, h_ref):
    c = pl.program_id(1)

    @pl.when(c == 0)
    def _():
        h_ref[...] = _rmsnorm_bf16(x_ref[...], g_ref[...])

    h = h_ref[...]
    v_ref[...] = _dot_w(h, wv_ref).astype(BF16)
    cu = _dot_w(h, wc_ref) * _dot_w(h, wu_ref)
    row = lax.broadcasted_iota(I32, cu.shape, 0)
    cu1 = jnp.where(row >= 1, pltpu.roll(cu, 1, 0), 0.0)
    cu2 = jnp.where(row >= 2, pltpu.roll(cu, 2, 0), 0.0)
    cw = cw_ref[...]
    y = cw[0:1, :] * cu2
    y = y + cw[1:2, :] * cu1
    y = y + cw[2:3, :] * cu
    y_conv = _dot_w(h, wb_ref) * y
    zc_ref[...] = (jax.nn.sigmoid(_dot_w(h, wgc_ref)) * y_conv).astype(BF16)
    sga_ref[...] = jax.nn.sigmoid(_dot_w(h, wga_ref)).astype(BF16)


def _conv_branch(x2, g, w_in, conv_w, layer, seq):
    n, d = x2.shape
    tn = _tile(d, 256)
    nc = d // tn

    def wspec(k):
        return pl.BlockSpec((None, d, tn), lambda b, c: (layer, 0, k * nc + c))

    out = jax.ShapeDtypeStruct((n, d), BF16)
    ospec = pl.BlockSpec((seq, tn), lambda b, c: (b, c))
    return pl.pallas_call(
        _conv_kernel,
        grid=(n // seq, nc),
        in_specs=[
            pl.BlockSpec((seq, d), lambda b, c: (b, 0)),
            pl.BlockSpec((1, d), lambda b, c: (0, 0)),
            wspec(2), wspec(3), wspec(4), wspec(5), wspec(6), wspec(7),
            pl.BlockSpec((None, CONV_K, tn), lambda b, c: (layer, 0, c)),
        ],
        out_specs=[ospec, ospec, ospec],
        out_shape=[out, out, out],
        scratch_shapes=[pltpu.VMEM((seq, d), BF16)],
        compiler_params=_params(("arbitrary", "arbitrary")),
        name="v_conv_gate",
    )(x2, g, w_in, w_in, w_in, w_in, w_in, w_in, conv_w)


ATTN_TQ = 512
ATTN_TK = 512


def _attn_kernel(lq1_ref, lk1_ref, lq2_ref, lk2_ref, q_ref, k_ref, v_ref, g_ref, o_ref,
                 *, tq, tk, lam_init):
    qi = pl.program_id(2)
    lam = (jnp.exp(jnp.sum(lq1_ref[...] * lk1_ref[...], axis=-1, keepdims=True))
           - jnp.exp(jnp.sum(lq2_ref[...] * lk2_ref[...], axis=-1, keepdims=True))
           + lam_init)

    q = q_ref[...]
    lane = lax.broadcasted_iota(I32, q.shape, 1)
    zero = jnp.zeros_like(q)
    map0 = (lane & (HEAD_DIM // 2)) == 0
    q_maps = (jnp.where(map0, q, zero), jnp.where(map0, zero, q))
    nt = (((1,), (1,)), ((), ()))

    def kv_rows(j):
        return pl.ds(pl.multiple_of(j * tk, tk), tk)

    def scores(j):
        k = k_ref[kv_rows(j), :]
        return [lax.dot_general(qm, k, nt, preferred_element_type=F32) for qm in q_maps]

    def update(state, s, v):
        m, l, acc = state
        m_new = jnp.maximum(m, jnp.max(s, axis=-1, keepdims=True))
        alpha = jnp.exp2(m - m_new)
        p = jnp.exp2(s - m_new)
        l = alpha * l + jnp.sum(p, axis=-1, keepdims=True)
        acc = alpha * acc + jnp.dot(p.astype(BF16), v, preferred_element_type=F32)
        return m_new, l, acc

    def body(j, states):
        v = v_ref[kv_rows(j), :]
        return tuple(update(st, s, v) for st, s in zip(states, scores(j)))

    init = (jnp.full((tq, 1), -jnp.inf, F32), jnp.zeros((tq, 1), F32),
            jnp.zeros((tq, V_HEAD_DIM), F32))
    n_full = (qi * tq) // tk
    states = lax.fori_loop(0, n_full, body, (init, init))

    row = qi * tq + lax.broadcasted_iota(I32, (tq, tk), 0)
    col = lax.broadcasted_iota(I32, (tq, tk), 1)
    neg = jnp.finfo(F32).min
    for d in range(max(1, tq // tk)):
        j = n_full + d
        v = v_ref[kv_rows(j), :]
        states = tuple(update(st, jnp.where(j * tk + col <= row, s, neg), v)
                       for st, s in zip(states, scores(j)))

    (_, l1, a1), (_, l2, a2) = states
    o = a1 / l1 - lam * (a2 / l2)
    var = jnp.mean(o * o, axis=-1, keepdims=True)
    y = (o * lax.rsqrt(var + RMS_EPS) * g_ref[...]) * (1.0 - lam_init)
    o_ref[...] = y.astype(BF16)


def _attention(qk, v, lq1, lk1, lq2, lk2, subln_g, batch, seq, n_heads, lam_init):
    n = qk.shape[0]
    tq = _tile(seq, ATTN_TQ)
    tk = _tile(seq, ATTN_TK)
    nq = seq // tq
    kern = functools.partial(_attn_kernel, tq=tq, tk=tk, lam_init=lam_init)
    lspec = pl.BlockSpec((1, HEAD_DIM), lambda b, h, i: (0, 0))
    return pl.pallas_call(
        kern,
        grid=(batch, n_heads, nq),
        in_specs=[
            lspec, lspec, lspec, lspec,
            pl.BlockSpec((tq, V_HEAD_DIM), lambda b, h, i: (b * nq + i, h)),
            pl.BlockSpec((seq, V_HEAD_DIM), lambda b, h, i: (b, n_heads + h)),
            pl.BlockSpec((seq, V_HEAD_DIM), lambda b, h, i: (b, h)),
            pl.BlockSpec((1, V_HEAD_DIM), lambda b, h, i: (0, 0)),
        ],
        out_specs=pl.BlockSpec((tq, V_HEAD_DIM), lambda b, h, i: (b * nq + i, h)),
        out_shape=jax.ShapeDtypeStruct((n, n_heads * V_HEAD_DIM), BF16),
        compiler_params=_params(("arbitrary", "arbitrary", "arbitrary")),
        name="diff_attention",
    )(lq1, lk1, lq2, lk2, qk, qk, v, subln_g)


def _outproj_kernel(x_ref, ya_ref, sga_ref, zc_ref, w_ref, o_ref):
    m = sga_ref[...].astype(F32) * ya_ref[...].astype(F32) + zc_ref[...].astype(F32)
    o_ref[...] = x_ref[...] + _dot_w(m.astype(BF16), w_ref)


def _out_proj(x2, y_attn, sga, zc, w_out, layer):
    n, d = x2.shape
    tm = _tile(n, 512)
    row = pl.BlockSpec((tm, d), lambda i: (i, 0))
    return pl.pallas_call(
        _outproj_kernel,
        grid=(n // tm,),
        in_specs=[row, row, row, row, pl.BlockSpec((None, d, d), lambda i: (layer, 0, 0))],
        out_specs=row,
        out_shape=jax.ShapeDtypeStruct((n, d), F32),
        compiler_params=_params(("arbitrary",)),
        name="out_proj",
    )(x2, y_attn, sga, zc, w_out)


def _ffn_kernel(x_ref, g_ref, wg_ref, wu_ref, wd_ref, o_ref, h_ref, acc_ref, *, nf):
    f = pl.program_id(1)

    @pl.when(f == 0)
    def _():
        h_ref[...] = _rmsnorm_bf16(x_ref[...], g_ref[...])
        acc_ref[...] = jnp.zeros_like(acc_ref)

    h = h_ref[...]
    gate = _dot_w(h, wg_ref)
    up = _dot_w(h, wu_ref)
    a = (gate * jax.nn.sigmoid(gate) * up).astype(BF16)
    acc_ref[...] += _dot_w(a, wd_ref)

    @pl.when(f == nf - 1)
    def _():
        o_ref[...] = x_ref[...] + acc_ref[...]


def _ffn_dense(x2, g, w_gate, w_up, w_down, layer):
    n, d = x2.shape
    ff = w_gate.shape[2]
    tm = _tile(n, 1024)
    tf = _tile(ff, 256)
    nf = ff // tf
    return pl.pallas_call(
        functools.partial(_ffn_kernel, nf=nf),
        grid=(n // tm, nf),
        in_specs=[
            pl.BlockSpec((tm, d), lambda i, f: (i, 0)),
            pl.BlockSpec((1, d), lambda i, f: (0, 0)),
            pl.BlockSpec((None, d, tf), lambda i, f: (layer, 0, f)),
            pl.BlockSpec((None, d, tf), lambda i, f: (layer, 0, f)),
            pl.BlockSpec((None, tf, d), lambda i, f: (layer, f, 0)),
        ],
        out_specs=pl.BlockSpec((tm, d), lambda i, f: (i, 0)),
        out_shape=jax.ShapeDtypeStruct((n, d), F32),
        scratch_shapes=[pltpu.VMEM((tm, d), BF16), pltpu.VMEM((tm, d), F32)],
        compiler_params=_params(("arbitrary", "arbitrary")),
        name="ffn_dense",
    )(x2, g, w_gate, w_up, w_down)


_R_E1, _R_E2, _R_RANK1, _R_RANK2, _R_W1, _R_W2 = range(6)
MOE_BLOCK = 256


def _router_kernel(x_ref, g_ref, whi_ref, wlo_ref, h_ref, info_ref, base_ref, cnt_ref,
                   tri_ref, carry_ref):
    i = pl.program_id(0)
    tm = x_ref.shape[0]

    @pl.when(i == 0)
    def _():
        r = lax.broadcasted_iota(I32, (tm, tm), 0)
        c = lax.broadcasted_iota(I32, (tm, tm), 1)
        tri_ref[...] = jnp.where(c < r, 1.0, 0.0).astype(BF16)
        carry_ref[...] = jnp.zeros_like(carry_ref)

    x = x_ref[...]
    var = jnp.mean(x * x, axis=-1, keepdims=True)
    h = x * lax.rsqrt(var + RMS_EPS) * g_ref[...]

    h_hi = h.astype(BF16)
    h_ref[...] = h_hi
    h_lo = (h - h_hi.astype(F32)).astype(BF16)
    dot = functools.partial(jnp.dot, preferred_element_type=F32)
    logits = dot(h_hi, whi_ref[...]) + dot(h_lo, whi_ref[...]) + dot(h_hi, wlo_ref[...])

    lane = lax.broadcasted_iota(I32, (tm, LANES), 1)
    ninf = -jnp.inf
    lg = jnp.where(lane < N_EXPERTS, logits, ninf)
    m1 = jnp.max(lg, axis=-1, keepdims=True)
    e1 = jnp.min(jnp.where(lg == m1, lane, LANES), axis=-1, keepdims=True)
    lg2 = jnp.where(lane == e1, ninf, lg)
    m2 = jnp.max(lg2, axis=-1, keepdims=True)
    e2 = jnp.min(jnp.where(lg2 == m2, lane, LANES), axis=-1, keepdims=True)
    ex2 = jnp.exp(m2 - m1)
    den = 1.0 + ex2
    w1 = 1.0 / den
    w2 = ex2 / den

    oh1 = lane == e1
    oh2 = lane == e2
    oh = jnp.where(oh1 | oh2, 1.0, 0.0)
    carry = carry_ref[...]
    base_ref[...] = carry
    before = dot(tri_ref[...], oh.astype(BF16))
    rank1 = jnp.sum(jnp.where(oh1, before, 0.0), axis=-1, keepdims=True)
    rank2 = jnp.sum(jnp.where(oh2, before, 0.0), axis=-1, keepdims=True)
    new_carry = carry + jnp.sum(oh, axis=0, keepdims=True)
    carry_ref[...] = new_carry
    cnt_ref[...] = new_carry

    info = jnp.zeros((tm, LANES), F32)
    for k, val in ((_R_E1, e1.astype(F32)), (_R_E2, e2.astype(F32)), (_R_RANK1, rank1),
                   (_R_RANK2, rank2), (_R_W1, w1), (_R_W2, w2)):
        info = jnp.where(lane == k, val, info)
    info_ref[...] = info


def _router(x2, g, w_hi, w_lo, tm):
    n, d = x2.shape
    return pl.pallas_call(
        _router_kernel,
        grid=(n // tm,),
        in_specs=[
            pl.BlockSpec((tm, d), lambda i: (i, 0)),
            pl.BlockSpec((1, d), lambda i: (0, 0)),
            pl.BlockSpec((d, LANES), lambda i: (0, 0)),
            pl.BlockSpec((d, LANES), lambda i: (0, 0)),
        ],
        out_specs=[
            pl.BlockSpec((tm, d), lambda i: (i, 0)),
            pl.BlockSpec((tm, LANES), lambda i: (i, 0)),
            pl.BlockSpec((None, 8, LANES), lambda i: (i, 0, 0)),
            pl.BlockSpec((8, LANES), lambda i: (0, 0)),
        ],
        out_shape=[
            jax.ShapeDtypeStruct((n, d), BF16),
            jax.ShapeDtypeStruct((n, LANES), F32),
            jax.ShapeDtypeStruct((n // tm, 8, LANES), F32),
            jax.ShapeDtypeStruct((8, LANES), F32),
        ],
        scratch_shapes=[pltpu.VMEM((tm, tm), BF16), pltpu.VMEM((8, LANES), F32)],
        compiler_params=_params(("arbitrary",)),
        name="moe_router",
    )(x2, g, w_hi, w_lo)


def _rows_to_tiles(ref, x):
    nl = x.shape[1] // LANES
    for s in range(nl):
        ref[pl.ds(s, x.shape[0], stride=nl), :] = x[:, s * LANES:(s + 1) * LANES]


def _tiles_to_rows(ref, d):
    nl = d // LANES
    r = ref.shape[0] // nl
    return jnp.concatenate([ref[pl.ds(s, r, stride=nl), :] for s in range(nl)], axis=1)


def _block_onehot(e, e1, e2, r1, r2, rank_iota):
    rank = jnp.where(e1 == e, r1, jnp.where(e2 == e, r2, -1.0))
    return jnp.where(rank_iota == rank, 1.0, 0.0).astype(BF16)


def _dispatch_kernel(win_ref, fill_ref, h_ref, info_ref, xs_hbm, w_ref, zero_ref, sem,
                     *, tb, n_steps, rows):
    i = pl.program_id(0)
    slot = i % 2
    nl = h_ref.shape[1] // LANES

    def rows_at(start):
        return xs_hbm.at[pl.ds(pl.multiple_of(start * nl, nl), tb * nl), :]

    def window(s, e, step):
        return pltpu.make_async_copy(w_ref.at[s, e], rows_at(win_ref[step * N_EXPERTS + e]),
                                     sem.at[s])

    info_t = info_ref[...].T
    e1, e2 = info_t[_R_E1:_R_E1 + 1, :], info_t[_R_E2:_R_E2 + 1, :]
    r1, r2 = info_t[_R_RANK1:_R_RANK1 + 1, :], info_t[_R_RANK2:_R_RANK2 + 1, :]
    rank_iota = lax.broadcasted_iota(I32, (tb, tb), 0).astype(F32)
    h = h_ref[...]
    for e in range(N_EXPERTS):
        sel = _block_onehot(e, e1, e2, r1, r2, rank_iota)
        _rows_to_tiles(w_ref.at[slot, e], jnp.dot(sel, h, preferred_element_type=F32))

    @pl.when(i > 0)
    def _():
        for e in range(N_EXPERTS):
            window(1 - slot, e, i - 1).wait()

    for e in range(N_EXPERTS):
        window(slot, e, i).start()

    @pl.when(i == n_steps - 1)
    def _():
        for e in range(N_EXPERTS):
            window(slot, e, i).wait()
        zero_ref[...] = jnp.zeros_like(zero_ref)

        def zero_window(start):
            return pltpu.make_async_copy(zero_ref, rows_at(start), sem.at[2])

        def zero_range(lo, hi):
            n_win = (hi - lo) // tb
            lax.fori_loop(0, n_win, lambda k, c: (zero_window(lo + k * tb).start(), c)[1], 0)
            lax.fori_loop(0, n_win, lambda k, c: (zero_window(0).wait(), c)[1], 0)

            @pl.when(lo + n_win * tb < hi)
            def _():
                zero_window(hi - tb).start()
                zero_window(0).wait()

        for e in range(N_EXPERTS):
            zero_range(fill_ref[2 * e], fill_ref[2 * e + 1])
        zero_range(fill_ref[2 * N_EXPERTS - 1], rows)


def _dispatch(win, fill, h2, info, rows, tb):
    n, d = h2.shape
    nl = d // LANES
    nblk = n // tb
    kern = functools.partial(_dispatch_kernel, tb=tb, n_steps=nblk, rows=rows)
    return pl.pallas_call(
        kern,
        grid_spec=pltpu.PrefetchScalarGridSpec(
            num_scalar_prefetch=2,
            grid=(nblk,),
            in_specs=[
                pl.BlockSpec((tb, d), lambda i, win, fill: (i, 0)),
                pl.BlockSpec((tb, LANES), lambda i, win, fill: (i, 0)),
            ],
            out_specs=pl.BlockSpec(memory_space=pl.ANY),
            scratch_shapes=[pltpu.VMEM((2, N_EXPERTS, tb * nl, LANES), F32),
                            pltpu.VMEM((tb * nl, LANES), F32),
                            pltpu.SemaphoreType.DMA((3,))],
        ),
        out_shape=jax.ShapeDtypeStruct((rows * nl, LANES), F32),
        compiler_params=_params(("arbitrary",)),
        name="moe_dispatch",
    )(win, fill, h2, info)


def _experts_kernel(te_ref, tblk_ref, tv_ref, x_ref, wg_ref, wu_ref, wd_ref, o_ref,
                    xb_ref, acc_ref, *, nf, sub):
    t = pl.program_id(0)
    f = pl.program_id(1)
    n_rows = tv_ref[t]
    valid = n_rows > 0

    @pl.when(valid & (f == 0))
    def _():
        xb_ref[...] = _tiles_to_rows(x_ref, xb_ref.shape[1]).astype(BF16)
        acc_ref[...] = jnp.zeros_like(acc_ref)

    for s in range(xb_ref.shape[0] // sub):
        @pl.when(n_rows > s * sub)
        def _():
            rows = slice(s * sub, (s + 1) * sub)
            xb = xb_ref[rows, :]
            gate = _dot_w(xb, wg_ref)
            up = _dot_w(xb, wu_ref)
            a = (gate * jax.nn.sigmoid(gate) * up).astype(BF16)
            acc_ref[rows, :] += _dot_w(a, wd_ref)

    @pl.when(valid & (f == nf - 1))
    def _():
        _rows_to_tiles(o_ref, acc_ref[...])

    @pl.when(jnp.logical_not(valid) & (f == nf - 1))
    def _():
        o_ref[...] = jnp.zeros_like(o_ref)


def _experts(tile_expert, tile_block, tile_valid, xs, w_gate, w_up, w_down, layer, tm):
    d, ff = w_gate.shape[2], w_gate.shape[3]
    nl = d // LANES
    tf = _tile(ff, 512)
    nf = ff // tf
    n_tiles = tile_expert.shape[0]

    def fidx(t, f, tv):
        return jnp.where(tv[t] > 0, f, nf - 1)

    return pl.pallas_call(
        functools.partial(_experts_kernel, nf=nf, sub=_tile(tm, 512)),
        grid_spec=pltpu.PrefetchScalarGridSpec(
            num_scalar_prefetch=3,
            grid=(n_tiles, nf),
            in_specs=[
                pl.BlockSpec((tm * nl, LANES), lambda t, f, te, tb, tv: (t, 0)),
                pl.BlockSpec((None, None, d, tf),
                             lambda t, f, te, tb, tv: (layer, te[t], 0, fidx(t, f, tv))),
                pl.BlockSpec((None, None, d, tf),
                             lambda t, f, te, tb, tv: (layer, te[t], 0, fidx(t, f, tv))),
                pl.BlockSpec((None, None, tf, d),
                             lambda t, f, te, tb, tv: (layer, te[t], fidx(t, f, tv), 0)),
            ],
            out_specs=pl.BlockSpec((tm * nl, LANES), lambda t, f, te, tb, tv: (t, 0)),
            scratch_shapes=[pltpu.VMEM((tm, d), BF16), pltpu.VMEM((tm, d), F32)],
        ),
        out_shape=jax.ShapeDtypeStruct(xs.shape, F32),
        compiler_params=_params(("arbitrary", "arbitrary")),
        name="moe_experts",
    )(tile_expert, tile_block, tile_valid, xs, w_gate, w_up, w_down)


def _combine_kernel(win_ref, x_ref, info_ref, gfin_ref, ys_hbm, o_ref, v_ref, sem,
                    *, tb, n_steps, final_norm):
    i = pl.program_id(0)
    slot = i % 2

    d = x_ref.shape[1]
    nl = d // LANES

    def window(s, e, step):
        start = win_ref[step * N_EXPERTS + e]
        src = ys_hbm.at[pl.ds(pl.multiple_of(start * nl, nl), tb * nl), :]
        return pltpu.make_async_copy(src, v_ref.at[s, e], sem.at[s])

    @pl.when(i == 0)
    def _():
        for e in range(N_EXPERTS):
            window(0, e, 0).start()

    @pl.when(i + 1 < n_steps)
    def _():
        for e in range(N_EXPERTS):
            window(1 - slot, e, i + 1).start()

    for e in range(N_EXPERTS):
        window(slot, e, i).wait()

    info = info_ref[...]
    col = lambda k: info[:, k:k + 1]
    e1, e2, r1, r2 = col(_R_E1), col(_R_E2), col(_R_RANK1), col(_R_RANK2)
    rank_iota = lax.broadcasted_iota(I32, (tb, tb), 1).astype(F32)
    none = jnp.full_like(e1, -1.0)
    y = jnp.zeros((2 * tb, d), F32)
    for e in range(N_EXPERTS):
        sel = jnp.concatenate([_block_onehot(e, e1, none, r1, r1, rank_iota),
                               _block_onehot(e, e2, none, r2, r2, rank_iota)], axis=0)
        v = _tiles_to_rows(v_ref.at[slot, e], d).astype(BF16)
        y = y + jnp.dot(sel, v, preferred_element_type=F32)
    out = x_ref[...] + (col(_R_W1) * y[:tb] + col(_R_W2) * y[tb:])
    if final_norm:
        var = jnp.mean(out * out, axis=-1, keepdims=True)
        out = out * lax.rsqrt(var + RMS_EPS) * gfin_ref[...]
    o_ref[...] = out


def _combine(win, x2, info, g_final, ys, tb, final_norm):
    n, d = x2.shape
    nblk = n // tb
    kern = functools.partial(_combine_kernel, tb=tb, n_steps=nblk, final_norm=final_norm)
    return pl.pallas_call(
        kern,
        grid_spec=pltpu.PrefetchScalarGridSpec(
            num_scalar_prefetch=1,
            grid=(nblk,),
            in_specs=[
                pl.BlockSpec((tb, d), lambda i, win: (i, 0)),
                pl.BlockSpec((tb, LANES), lambda i, win: (i, 0)),
                pl.BlockSpec((1, d), lambda i, win: (0, 0)),
                pl.BlockSpec(memory_space=pl.ANY),
            ],
            out_specs=pl.BlockSpec((tb, d), lambda i, win: (i, 0)),
            scratch_shapes=[pltpu.VMEM((2, N_EXPERTS, tb * (d // LANES), LANES), F32),
                            pltpu.SemaphoreType.DMA((2,))],
        ),
        out_shape=jax.ShapeDtypeStruct((n, d), F32),
        compiler_params=_params(("arbitrary",)),
        name="moe_combine",
    )(win, x2, info, g_final, ys)


def _final_norm_kernel(x_ref, g_ref, o_ref):
    x = x_ref[...]
    var = jnp.mean(x * x, axis=-1, keepdims=True)
    o_ref[...] = x * lax.rsqrt(var + RMS_EPS) * g_ref[...]


def _final_norm(x2, g):
    n, d = x2.shape
    tm = _tile(n, 1024)
    return pl.pallas_call(
        _final_norm_kernel,
        grid=(n // tm,),
        in_specs=[pl.BlockSpec((tm, d), lambda i: (i, 0)), pl.BlockSpec((1, d), lambda i: (0, 0))],
        out_specs=pl.BlockSpec((tm, d), lambda i: (i, 0)),
        out_shape=jax.ShapeDtypeStruct((n, d), F32),
        compiler_params=_params(("arbitrary",)),
        name="final_norm",
    )(x2, g)


def _moe(x2, g, w_router, w_gate, w_up, w_down, layer, g_final, final_norm):
    n, d = x2.shape
    tb = _tile(n, MOE_BLOCK)
    tm = _tile(n, 1024)
    n_tiles = -(-(2 * n + N_EXPERTS * tb) // tm) + N_EXPERTS
    w_pad = jnp.zeros((d, LANES), F32).at[:, :N_EXPERTS].set(w_router)
    w_hi = w_pad.astype(BF16)
    w_lo = (w_pad - w_hi.astype(F32)).astype(BF16)
    h2, info, base, cnt = _router(x2, g, w_hi, w_lo, tb)

    counts = cnt[0, :N_EXPERTS].astype(I32)
    region_tiles = (counts + tb + tm - 1) // tm
    tile_ends = jnp.cumsum(region_tiles)
    off = (tile_ends - region_tiles) * tm
    win = (off[None, :] + base[:, 0, :N_EXPERTS].astype(I32)).reshape(-1)
    fill = jnp.stack([off + counts, tile_ends * tm], axis=1).reshape(-1)
    xs = _dispatch(win, fill, h2, info, n_tiles * tm, tb)

    t_idx = jnp.arange(n_tiles, dtype=I32)
    tile_expert = jnp.minimum(jnp.sum((t_idx[:, None] >= tile_ends[None, :]).astype(I32), axis=1),
                              N_EXPERTS - 1)
    tile_rows = jnp.clip((off + counts)[tile_expert] - t_idx * tm, 0, tm)
    ys = _experts(tile_expert, t_idx, tile_rows, xs, w_gate, w_up, w_down, layer, tm)
    return _combine(win, x2, info, g_final, ys, tb, final_norm)


def _rope_tables(seq):
    pos = jnp.arange(seq, dtype=F32)
    inv = 1.0 / (ROPE_THETA ** (jnp.arange(0, HEAD_DIM, 2, dtype=F32) / HEAD_DIM))
    ang = pos[:, None] * inv[None, :]
    reps = LANES // (HEAD_DIM // 2)
    cos = jnp.tile(jnp.cos(ang), (1, reps))
    sin = jnp.tile(jnp.sin(ang), (1, reps))
    sign = jnp.where(jnp.arange(LANES) < LANES // 2, -1.0, 1.0).astype(F32)
    return cos, sin * sign


def kernel(x, norm_mix, w_in, conv_w, lam_q1, lam_k1, lam_q2, lam_k2, subln_g, w_out, norm_ffn,
           ffn_w_gate, ffn_w_up, ffn_w_down, router_w, exp_w_gate, exp_w_up, exp_w_down,
           norm_final):
    batch, seq, d = x.shape
    depth = w_in.shape[0]
    n_heads = d // V_HEAD_DIM
    n = batch * seq
    assert w_in.shape[2] == 8 * d and seq % 8 == 0 and d % LANES == 0
    cos_t, sin_t = _rope_tables(seq)
    qk_perm = _head_lane_perm(2 * n_heads)
    x2 = x.reshape(n, d)
    row = lambda v: v.reshape(1, -1)

    for l in range(depth):
        lam_init = 0.8 - 0.6 * math.exp(-0.3 * l)
        w_qk = jnp.take(w_in[l, :, :2 * d], qk_perm, axis=1).astype(BF16)
        g_mix = row(norm_mix[l])
        qk = _qk_proj(x2, g_mix, w_qk, cos_t, sin_t, seq)
        v, zc, sga = _conv_branch(x2, g_mix, w_in, conv_w, l, seq)
        y_attn = _attention(qk, v, row(lam_q1[l]), row(lam_k1[l]), row(lam_q2[l]), row(lam_k2[l]),
                            row(subln_g[l]), batch, seq, n_heads, lam_init)
        x2 = _out_proj(x2, y_attn, sga, zc, w_out, l)
        g_ffn = row(norm_ffn[l])
        j = l // 2
        last = l == depth - 1
        if l % 2 == 0:
            x2 = _ffn_dense(x2, g_ffn, ffn_w_gate, ffn_w_up, ffn_w_down, j)
            if last:
                x2 = _final_norm(x2, row(norm_final))
        else:
            x2 = _moe(x2, g_ffn, router_w[j], exp_w_gate, exp_w_up, exp_w_down, j,
                      row(norm_final), last)
    return x2.reshape(batch, seq, d)
```
